```python
import jax, jax.numpy as jnp
from jax import lax
import numpy as np

D_MODEL = 1024
BATCH = 8
SEQ = 2048
DEPTH = 1
DEC_BATCH = 128
DEC_SEQ = 8
PAST_LEN = 16384
PAGE_SIZE = 128

RW_HEADS = 8
RW_HEAD_DIM = 64
RW_WIDTH = RW_HEADS * RW_HEAD_DIM
RW_DECAY_RANK = 64
RW_AAA_RANK = 64
RW_GATE_RANK = 128
RW_COLS = 3 * RW_WIDTH + RW_DECAY_RANK + RW_AAA_RANK + RW_GATE_RANK
RW_GN_EPS = 64e-5
HG_HEADS = 4
HG_DK = 128
HG_DV = 128
HG_FDIM = HG_HEADS * HG_DK
HG_VDIM = HG_HEADS * HG_DV
HG_COLS = 2 * HG_FDIM + 2 * HG_VDIM
HG_CHUNK = 32
GATE_COLS = 2 * D_MODEL
IN_COLS = RW_COLS + HG_COLS + GATE_COLS
PEER_KEYS = 128
PEER_EXPERTS = PEER_KEYS * PEER_KEYS
PEER_HEADS = 8
PEER_TOPK = 16
PEER_QDIM = 256
PEER_HALF = PEER_QDIM // 2
PEER_BLOCK = 128
NORM_EPS = 1e-6

kernel_name = "rwkv7_hgrn2_peer_hybrid_step"


def _rmsnorm(x, g):
    x32 = x.astype(jnp.float32)
    y = x32 * lax.rsqrt(jnp.mean(x32 * x32, axis=-1, keepdims=True) + NORM_EPS)
    return (y * g.astype(jnp.float32)).astype(x.dtype)


def _rwkv7(zs, wkv0, w0, w2, a0, a2, g2, k_k, k_a, r_k, ln_w, ln_b):
    B, T, _ = zs.shape
    f32 = jnp.float32
    zs = zs.astype(f32)
    r, k, v, d_low, a_low, g_low = jnp.split(
        zs, [RW_WIDTH, 2 * RW_WIDTH, 3 * RW_WIDTH, 3 * RW_WIDTH + RW_DECAY_RANK,
             3 * RW_WIDTH + RW_DECAY_RANK + RW_AAA_RANK], axis=-1)
    w_log = -jax.nn.softplus(-(w0.astype(f32) + jnp.tanh(d_low) @ w2.astype(f32))) - 0.5
    decay = jnp.exp(-jnp.exp(w_log))
    a = jax.nn.sigmoid(a0.astype(f32) + a_low @ a2.astype(f32))
    g = jax.nn.sigmoid(g_low) @ g2.astype(f32)

    def heads(t):
        return t.reshape(B, T, RW_HEADS, RW_HEAD_DIM)

    kk = heads(k * k_k.astype(f32))
    kk = kk / jnp.maximum(jnp.linalg.norm(kk, axis=-1, keepdims=True), 1e-12)
    k = k * (1.0 + (a - 1.0) * k_a.astype(f32))
    r_h, k_h, v_h, w_h, a_h = heads(r), heads(k), heads(v), heads(decay), heads(a)

    def step(S, inp):
        r_t, w_t, k_t, v_t, kk_t, a_t = inp
        sa = jnp.einsum('bhvk,bhk->bhv', S, -kk_t)
        S = (S * w_t[:, :, None, :] + sa[..., None] * (kk_t * a_t)[:, :, None, :]
             + v_t[..., None] * k_t[:, :, None, :])
        return S, jnp.einsum('bhvk,bhk->bhv', S, r_t)

    def tm(t):
        return jnp.moveaxis(t, 1, 0)

    S_fin, o = lax.scan(step, wkv0.astype(f32),
                        (tm(r_h), tm(w_h), tm(k_h), tm(v_h), tm(kk), tm(a_h)))
    o = jnp.moveaxis(o, 0, 1)
    mu = jnp.mean(o, axis=-1, keepdims=True)
    var = jnp.mean(jnp.square(o - mu), axis=-1, keepdims=True)
    o = ((o - mu) * lax.rsqrt(var + RW_GN_EPS)).reshape(B, T, RW_WIDTH)
    o = o * ln_w.astype(f32) + ln_b.astype(f32)
    bonus = jnp.sum(r_h * k_h * r_k.astype(f32), axis=-1, keepdims=True) * v_h
    o = (o + bonus.reshape(B, T, RW_WIDTH)) * g
    return o, S_fin


def _hgrn2(q, f_logit, i_in, g_out, hg0, lb, norm_g):
    B, T, _ = q.shape
    f32 = jnp.float32
    f = lb + (1.0 - lb) * jax.nn.sigmoid(f_logit.astype(f32))
    logf = jnp.log(f)
    k = 1.0 - f
    L = min(HG_CHUNK, T)
    n_ch = -(-T // L)
    pad = n_ch * L - T

    def chunks(t, d):
        t = jnp.pad(t.astype(f32), ((0, 0), (0, pad), (0, 0)))
        return t.reshape(B, n_ch, L, HG_HEADS, d).transpose(1, 0, 3, 2, 4)

    qc, kc, lc, vc = chunks(q, HG_DK), chunks(k, HG_DK), chunks(logf, HG_DK), chunks(i_in, HG_DV)
    b = jnp.cumsum(lc, axis=3)
    b_last = b[:, :, :, -1:, :]
    q_in = qc * jnp.exp(b)
    k_in = kc * jnp.exp(-b)
    k_out = kc * jnp.exp(b_last - b)
    causal = jnp.tril(jnp.ones((L, L), dtype=bool))
    A = jnp.where(causal, jnp.einsum('nbhld,nbhsd->nbhls', q_in, k_in), 0.0)
    o_intra = jnp.einsum('nbhls,nbhsv->nbhlv', A, vc)

    def step(S, inp):
        q_t, k_t, v_t, d_t = inp
        o_t = jnp.einsum('bhld,bhdv->bhlv', q_t, S)
        S = S * d_t[..., None] + jnp.einsum('bhsd,bhsv->bhdv', k_t, v_t)
        return S, o_t

    S_fin, o_inter = lax.scan(step, hg0.astype(f32),
                              (q_in, k_out, vc, jnp.exp(b_last[:, :, :, 0, :])))
    o = (o_intra + o_inter).transpose(1, 0, 3, 2, 4).reshape(B, n_ch * L, HG_HEADS, HG_DV)[:, :T]
    o = o * lax.rsqrt(jnp.mean(o * o, axis=-1, keepdims=True) + NORM_EPS) * norm_g.astype(f32)
    o = o.reshape(B, T, HG_VDIM) * jax.nn.silu(g_out.astype(f32))
    return o, S_fin


def _peer(xn, w_q, keys, u_tab, v_tab):
    B, T, D = xn.shape
    n_tok = B * T
    blk = min(PEER_BLOCK, n_tok)
    n_blk = -(-n_tok // blk)
    xf = jnp.pad(xn.reshape(n_tok, D), ((0, n_blk * blk - n_tok), (0, 0))).reshape(n_blk, blk, D)

    def one_block(xt):
        q = jnp.einsum('td,dc->tc', xt, w_q).reshape(blk, PEER_HEADS, 2, PEER_HALF)
        s = jnp.einsum('thcd,hckd->thck', q, keys).astype(jnp.float32)
        v1, i1 = lax.top_k(s[:, :, 0], PEER_TOPK)
        v2, i2 = lax.top_k(s[:, :, 1], PEER_TOPK)
        cand = (v1[..., :, None] + v2[..., None, :]).reshape(blk, PEER_HEADS, PEER_TOPK * PEER_TOPK)
        cv, ci = lax.top_k(cand, PEER_TOPK)
        idx = (jnp.take_along_axis(i1, ci // PEER_TOPK, axis=-1) * PEER_KEYS
               + jnp.take_along_axis(i2, ci % PEER_TOPK, axis=-1))
        gates = jax.nn.softmax(cv, axis=-1)
        act = jax.nn.gelu(jnp.einsum('thkd,td->thk', u_tab[idx], xt).astype(jnp.float32),
                          approximate=False)
        return jnp.einsum('thk,thkd->td', (gates * act).astype(xt.dtype), v_tab[idx])

    out = lax.map(one_block, xf)
    return out.reshape(n_blk * blk, D)[:n_tok].reshape(B, T, D)


def _layer(x, shift0, wkv0, hg0, lb, norm_mix_g, w_in, rw_mu, rw_w0, rw_w2, rw_a0, rw_a2, rw_g2,
           rw_k_k, rw_k_a, rw_r_k, rw_ln_w, rw_ln_b, hg_norm_g, w_up_a, w_up_b, w_out,
           norm_ffn_g, peer_w_q, peer_keys, peer_u, peer_v):
    xn = _rmsnorm(x, norm_mix_g)
    z = jnp.einsum('btd,dc->btc', xn, w_in)
    z_rw = z[..., :RW_COLS]
    z_first_prev = jnp.einsum('bd,dc->bc', shift0, w_in[:, :RW_COLS])
    z_rw_prev = jnp.concatenate([z_first_prev[:, None], z_rw[:, :-1]], axis=1)
    z_rw_shift = z_rw + (z_rw_prev - z_rw) * rw_mu
    o_a, wkv_new = _rwkv7(z_rw_shift, wkv0, rw_w0, rw_w2, rw_a0, rw_a2, rw_g2,
                          rw_k_k, rw_k_a, rw_r_k, rw_ln_w, rw_ln_b)
    z_hg = z[..., RW_COLS:RW_COLS + HG_COLS]
    hq, hf, hi, hgate = jnp.split(z_hg, [HG_FDIM, 2 * HG_FDIM, 2 * HG_FDIM + HG_VDIM], axis=-1)
    o_b, hg_new = _hgrn2(hq, hf, hi, hgate, hg0, lb, hg_norm_g)
    gate_a, gate_b = jnp.split(z[..., RW_COLS + HG_COLS:], 2, axis=-1)
    merged = (jax.nn.sigmoid(gate_a.astype(jnp.float32)) * (o_a @ w_up_a.astype(jnp.float32))
              + jax.nn.sigmoid(gate_b.astype(jnp.float32)) * (o_b @ w_up_b.astype(jnp.float32)))
    x = x + jnp.einsum('btd,de->bte', merged.astype(x.dtype), w_out)
    x = x + _peer(_rmsnorm(x, norm_ffn_g), peer_w_q, peer_keys, peer_u, peer_v)
    return x, xn[:, -1], wkv_new.astype(x.dtype), hg_new.astype(x.dtype)


def setup_inputs(seed: int = 0) -> dict:
    key = jax.random.key(seed)
    ks = jax.random.split(key, 32)
    f32 = jnp.float32

    def nrm(k, shape, scale):
        return jax.random.normal(k, shape, f32) * scale

    return {
        "x_prompt": nrm(ks[0], (BATCH, SEQ, D_MODEL), 1.0),
        "x_sample": nrm(ks[1], (DEC_BATCH, DEC_SEQ, D_MODEL), 1.0),
        "state_rwkv_shift": nrm(ks[2], (DEPTH, DEC_BATCH, D_MODEL), 1.0),
        "state_rwkv_wkv": nrm(ks[3], (DEPTH, DEC_BATCH, RW_HEADS, RW_HEAD_DIM, RW_HEAD_DIM), 0.3),
        "state_hgrn": nrm(ks[4], (DEPTH, DEC_BATCH, HG_HEADS, HG_DK, HG_DV), 0.3),
        "norm_mix_g": 1.0 + nrm(ks[5], (DEPTH, D_MODEL), 0.02),
        "w_in": nrm(ks[6], (DEPTH, D_MODEL, IN_COLS), D_MODEL ** -0.5),
        "rw_mu": jax.random.uniform(ks[7], (DEPTH, RW_COLS), f32),
        "rw_w0": nrm(ks[8], (DEPTH, RW_WIDTH), 0.5) - 0.5,
        "rw_w2": nrm(ks[9], (DEPTH, RW_DECAY_RANK, RW_WIDTH), 0.1),
        "rw_a0": nrm(ks[10], (DEPTH, RW_WIDTH), 0.1),
        "rw_a2": nrm(ks[11], (DEPTH, RW_AAA_RANK, RW_WIDTH), 0.5 * RW_AAA_RANK ** -0.5),
        "rw_g2": nrm(ks[12], (DEPTH, RW_GATE_RANK, RW_WIDTH), RW_GATE_RANK ** -0.5),
        "rw_k_k": 0.85 + nrm(ks[13], (DEPTH, RW_WIDTH), 0.05),
        "rw_k_a": 1.0 + nrm(ks[14], (DEPTH, RW_WIDTH), 0.05),
        "rw_r_k": nrm(ks[15], (DEPTH, RW_HEADS, RW_HEAD_DIM), 0.1),
        "rw_ln_w": 1.0 + nrm(ks[16], (DEPTH, RW_WIDTH), 0.02),
        "rw_ln_b": nrm(ks[17], (DEPTH, RW_WIDTH), 0.02),
        "hg_lb_logits": 1.0 + nrm(ks[18], (DEPTH + 1, HG_FDIM), 0.1),
        "hg_norm_g": 1.0 + nrm(ks[19], (DEPTH, HG_DV), 0.02),
        "w_up_a": nrm(ks[20], (DEPTH, RW_WIDTH, D_MODEL), RW_WIDTH ** -0.5),
        "w_up_b": nrm(ks[21], (DEPTH, HG_VDIM, D_MODEL), HG_VDIM ** -0.5),
        "w_out": nrm(ks[22], (DEPTH, D_MODEL, D_MODEL), D_MODEL ** -0.5),
        "norm_ffn_g": 1.0 + nrm(ks[23], (DEPTH, D_MODEL), 0.02),
        "peer_w_q": nrm(ks[24], (DEPTH, D_MODEL, PEER_HEADS * PEER_QDIM), D_MODEL ** -0.5),
        "peer_keys": nrm(ks[25], (DEPTH, PEER_HEADS, 2, PEER_KEYS, PEER_HALF), PEER_HALF ** -0.5),
        "peer_u": nrm(ks[26], (DEPTH, PEER_EXPERTS, D_MODEL), D_MODEL ** -0.5),
        "peer_v": nrm(ks[27], (DEPTH, PEER_EXPERTS, D_MODEL), 0.1),
        "norm_final_g": 1.0 + nrm(ks[28], (D_MODEL,), 0.02),
    }


def reference(x_prompt, x_sample, state_rwkv_shift, state_rwkv_wkv, state_hgrn,
              norm_mix_g, w_in, rw_mu, rw_w0, rw_w2, rw_a0, rw_a2, rw_g2, rw_k_k, rw_k_a,
              rw_r_k, rw_ln_w, rw_ln_b, hg_lb_logits, hg_norm_g, w_up_a, w_up_b, w_out,
              norm_ffn_g, peer_w_q, peer_keys, peer_u, peer_v, norm_final_g):
    b_p = x_prompt.shape[0]
    dt = x_prompt.dtype
    lbs = jnp.cumsum(jax.nn.softmax(hg_lb_logits.astype(jnp.float32), axis=0), axis=0)
    xp, xs = x_prompt, x_sample
    p_shift, p_wkv, p_hg, s_shift, s_wkv, s_hg = [], [], [], [], [], []
    for l in range(DEPTH):
        w_l = (norm_mix_g[l], w_in[l], rw_mu[l], rw_w0[l], rw_w2[l], rw_a0[l], rw_a2[l], rw_g2[l],
               rw_k_k[l], rw_k_a[l], rw_r_k[l], rw_ln_w[l], rw_ln_b[l], hg_norm_g[l],
               w_up_a[l], w_up_b[l], w_out[l], norm_ffn_g[l], peer_w_q[l], peer_keys[l],
               peer_u[l], peer_v[l])
        xp, sh, wk, hg = _layer(
            xp, jnp.zeros((b_p, D_MODEL), dt),
            jnp.zeros((b_p, RW_HEADS, RW_HEAD_DIM, RW_HEAD_DIM), dt),
            jnp.zeros((b_p, HG_HEADS, HG_DK, HG_DV), dt), lbs[l], *w_l)
        p_shift.append(sh)
        p_wkv.append(wk)
        p_hg.append(hg)
        xs, sh, wk, hg = _layer(xs, state_rwkv_shift[l], state_rwkv_wkv[l], state_hgrn[l], lbs[l], *w_l)
        s_shift.append(sh)
        s_wkv.append(wk)
        s_hg.append(hg)
    y_prompt = _rmsnorm(xp, norm_final_g)
    y_sample = _rmsnorm(xs, norm_final_g)
    return (y_prompt, y_sample, jnp.stack(p_shift), jnp.stack(p_wkv), jnp.stack(p_hg),
            jnp.stack(s_shift), jnp.stack(s_wkv), jnp.stack(s_hg))
```

```python
import functools

import numpy as np
import jax
import jax.numpy as jnp
from jax import lax
from jax.experimental import pallas as pl
from jax.experimental.pallas import tpu as pltpu

F32 = jnp.float32
BF16 = jnp.bfloat16

D_MODEL = 1024
RW_HEADS = 8
RW_HEAD_DIM = 64
RW_WIDTH = 512
RW_DECAY_RANK = 64
RW_AAA_RANK = 64
RW_GATE_RANK = 128
RW_COLS = 1792
RW_GN_EPS = 64e-5
HG_HEADS = 4
HG_DK = 128
HG_DV = 128
HG_COLS = 2048
HG_CHUNK = 32
GATE_COLS = 2048
PEER_KEYS = 128
PEER_HEADS = 8
PEER_TOPK = 16
PEER_QDIM = 256
NORM_EPS = 1e-6

LANES = 128
SEG = 2048
RW_PAD = 1920
SCAN_NB = 8
PEER_CHUNK = 256
VMEM_LIMIT = 56 * 1024 * 1024

_HM = (np.arange(RW_WIDTH) % RW_HEADS) * RW_HEAD_DIM + np.arange(RW_WIDTH) // RW_HEADS


def _cparams(sem):
    return pltpu.CompilerParams(dimension_semantics=sem, vmem_limit_bytes=VMEM_LIMIT)


def _sigmoid(x):
    return 1.0 / (1.0 + jnp.exp(-x))


def _rms(x, g):
    return x * lax.rsqrt(jnp.mean(x * x, axis=-1, keepdims=True) + NORM_EPS) * g


def _headsum(x):
    s = x[:, 0:128] + x[:, 128:256] + x[:, 256:384] + x[:, 384:512]
    for sh in (64, 32, 16, 8):
        s = s + pltpu.roll(s, sh, axis=1)
    return jnp.concatenate([s, s, s, s], axis=1)


def _dot(a, b):
    return jnp.dot(a, b, preferred_element_type=F32)


def _dot_nt(a, b):
    return lax.dot_general(a, b, (((1,), (1,)), ((), ())), preferred_element_type=F32)


def _exact_perm(x, p):
    x1 = x.astype(BF16)
    r1 = x - x1.astype(F32)
    x2 = r1.astype(BF16)
    x3 = (r1 - x2.astype(F32)).astype(BF16)
    return _dot(x1, p) + _dot(x2, p) + _dot(x3, p)


def _proj_body(x_ref, g_ref, w_ref, o_ref, *, do_norm):
    x = x_ref[...]
    if do_norm:
        x = _rms(x, g_ref[...])
    xb = x.astype(BF16)
    for c in range(0, SEG, 512):
        o_ref[:, c:c + 512] = _dot(xb, w_ref[:, c:c + 512])


def _proj(x, g, w, *, do_norm, nseg):
    n = x.shape[0]
    tm = min(512, n)
    return pl.pallas_call(
        functools.partial(_proj_body, do_norm=do_norm),
        grid=(nseg, n // tm),
        in_specs=[pl.BlockSpec((tm, D_MODEL), lambda c, i: (i, 0)),
                  pl.BlockSpec((1, D_MODEL), lambda c, i: (0, 0)),
                  pl.BlockSpec((D_MODEL, SEG), lambda c, i: (0, c))],
        out_specs=pl.BlockSpec((tm, SEG), lambda c, i: (i, c)),
        out_shape=jax.ShapeDtypeStruct((n, nseg * SEG), F32),
        compiler_params=_cparams(("arbitrary", "arbitrary")),
        name="proj",
    )(x, g, w)


def _rms_rows_body(x_ref, g_ref, o_ref):
    o_ref[...] = _rms(x_ref[...], g_ref[...])


def _rms_rows(x, g):
    return pl.pallas_call(
        _rms_rows_body,
        out_shape=jax.ShapeDtypeStruct(x.shape, F32),
        name="rms_rows",
    )(x, g)


def _rwkv_pre_body(z_ref, zp_ref, f_ref, mu_ref, w0_ref, w2_ref, a0_ref, a2_ref, g2_ref,
                   kk_ref, ka_ref, rk_ref,
                   nkk_o, w_o, bb_o, k2_o, r_o, v_o, g_o, bon_o):
    j = pl.program_id(1)
    z = z_ref[...]
    bt, tt, c = z.shape
    pos = lax.broadcasted_iota(jnp.int32, z.shape, 1)
    first = jnp.where(j == 0, f_ref[...], zp_ref[:, 7:8, :])
    zprev = jnp.where(pos == 0, first, pltpu.roll(z, 1, axis=1))
    zs = (z + (zprev - z) * mu_ref[...]).reshape(bt * tt, c)
    r = zs[:, 0:512]
    k = zs[:, 512:1024]
    v = zs[:, 1024:1536]
    dl = zs[:, 1536:1664]
    al = zs[:, 1664:1792]
    gl = zs[:, 1792:1920]
    y = w0_ref[...] + _dot(jnp.tanh(dl).astype(BF16), w2_ref[...])
    softplus_neg = jnp.maximum(-y, 0.0) + jnp.log(1.0 + jnp.exp(-jnp.abs(y)))
    decay = jnp.exp(-jnp.exp(-softplus_neg - 0.5))
    a = _sigmoid(a0_ref[...] + _dot(al.astype(BF16), a2_ref[...]))
    g = _dot(_sigmoid(gl).astype(BF16), g2_ref[...])
    kk = k * kk_ref[...]
    kk = kk / jnp.maximum(jnp.sqrt(_headsum(kk * kk)), 1e-12)
    k2 = k * (1.0 + (a - 1.0) * ka_ref[...])
    bonus = _headsum(r * k2 * rk_ref[...]) * v
    shp = (bt, tt, RW_WIDTH)
    nkk_o[...] = (-kk).reshape(shp)
    w_o[...] = decay.reshape(shp)
    bb_o[...] = (kk * a).reshape(shp)
    k2_o[...] = k2.reshape(shp)
    r_o[...] = r.reshape(shp)
    v_o[...] = v.reshape(shp)
    g_o[...] = g.reshape(shp)
    bon_o[...] = bonus.reshape(shp)


def _rwkv_pre(z3, zfirst, p):
    b, t, _ = z3.shape
    if t >= 256:
        bt, tt = 1, 256
    else:
        bt, tt = min(b, 256 // t), t
    tpb = tt // 8
    row = lambda bi, j: (bi, j, 0)
    const2 = lambda bi, j: (0, 0)
    vec = pl.BlockSpec((1, RW_WIDTH), const2)
    low = pl.BlockSpec((LANES, RW_WIDTH), const2)
    out_spec = pl.BlockSpec((bt, tt, RW_WIDTH), row)
    out_shape = jax.ShapeDtypeStruct((b, t, RW_WIDTH), F32)
    return pl.pallas_call(
        _rwkv_pre_body,
        grid=(b // bt, t // tt),
        in_specs=[pl.BlockSpec((bt, tt, SEG), row),
                  pl.BlockSpec((bt, 8, SEG), lambda bi, j: (bi, jnp.maximum(j * tpb - 1, 0), 0)),
                  pl.BlockSpec((bt, 1, SEG), lambda bi, j: (bi, 0, 0)),
                  pl.BlockSpec((1, 1, SEG), lambda bi, j: (0, 0, 0)),
                  vec, low, vec, low, low, vec, vec, vec],
        out_specs=[out_spec] * 8,
        out_shape=[out_shape] * 8,
        compiler_params=_cparams(("arbitrary", "arbitrary")),
        name="rwkv_pre",
    )(z3, z3, zfirst, p["mu"], p["w0"], p["w2"], p["a0"], p["a2"], p["g2"],
      p["k_k"], p["k_a"], p["r_k"])


def _rwkv_scan_body(nkk_ref, w_ref, bb_ref, k2_ref, r_ref, v_ref, s0_ref, rep_ref, pm_ref, pmt_ref,
                    o_ref, sf_ref, s_scr, *, tc):
    j = pl.program_id(1)

    @pl.when(j == 0)
    def _():
        for b in range(SCAN_NB):
            s_scr[b] = _exact_perm(s0_ref[b], pm_ref[...])

    shape = (RW_HEAD_DIM, RW_WIDTH)
    diag = (lax.broadcasted_iota(jnp.int32, shape, 1) >> 3) == lax.broadcasted_iota(jnp.int32, shape, 0)

    def tile4(x):
        return jnp.concatenate([x, x, x, x], axis=1)

    def lanesum(x):
        s = x[:, 0:128] + x[:, 128:256] + x[:, 256:384] + x[:, 384:512]
        for sh in (64, 32, 16, 8):
            s = s + pltpu.roll(s, sh, axis=1)
        return s

    def step(t, carry):
        for b in range(SCAN_NB):
            nkk = nkk_ref[b, pl.ds(t, 1), :]
            wt = w_ref[b, pl.ds(t, 1), :]
            bbt = bb_ref[b, pl.ds(t, 1), :]
            k2t = k2_ref[b, pl.ds(t, 1), :]
            rt = r_ref[b, pl.ds(t, 1), :]
            vt = v_ref[b, pl.ds(t, 1), :]
            s = s_scr[b]
            sa = tile4(lanesum(s * nkk))
            vcol = tile4(_dot(jnp.where(diag, vt, 0.0).astype(BF16), rep_ref[...]))
            s = s * wt + sa * bbt + vcol * k2t
            s_scr[b] = s
            oc = tile4(lanesum(s * rt))
            o_ref[b, pl.ds(t, 1), :] = jnp.sum(jnp.where(diag, oc, 0.0), axis=0, keepdims=True)
        return carry

    lax.fori_loop(0, tc, step, 0)

    @pl.when(j == pl.num_programs(1) - 1)
    def _():
        for b in range(SCAN_NB):
            sf_ref[b] = _exact_perm(s_scr[b], pmt_ref[...])


def _rwkv_scan(rows, s0, consts):
    b, t, _ = rows[0].shape
    tc = min(64, t)
    row = pl.BlockSpec((SCAN_NB, tc, RW_WIDTH), lambda g, j: (g, j, 0))
    st = pl.BlockSpec((SCAN_NB, RW_HEAD_DIM, RW_WIDTH), lambda g, j: (g, 0, 0))
    const2 = lambda g, j: (0, 0)
    return pl.pallas_call(
        functools.partial(_rwkv_scan_body, tc=tc),
        grid=(b // SCAN_NB, t // tc),
        in_specs=[row] * 6 + [st,
                              pl.BlockSpec((RW_WIDTH, LANES), const2),
                              pl.BlockSpec((RW_WIDTH, RW_WIDTH), const2),
                              pl.BlockSpec((RW_WIDTH, RW_WIDTH), const2)],
        out_specs=[row, st],
        out_shape=[jax.ShapeDtypeStruct((b, t, RW_WIDTH), F32),
                   jax.ShapeDtypeStruct((b, RW_HEAD_DIM, RW_WIDTH), F32)],
        scratch_shapes=[pltpu.VMEM((SCAN_NB, RW_HEAD_DIM, RW_WIDTH), F32)],
        compiler_params=_cparams(("arbitrary", "arbitrary")),
        name="rwkv_scan",
    )(*rows, s0, consts["rep"], consts["pm"], consts["pmt"])


def _hgrn_body(q_ref, f_ref, i_ref, g_ref, lbl_ref, ng_ref, st_ref, o_ref, so_ref,
               qin_s, kin_s, kout_s, d_s, *, chunk, per_seq, layer):
    rows = q_ref.shape[0]
    lg = lbl_ref[...]
    e = jnp.exp(lg - jnp.max(lg, axis=0, keepdims=True))
    sm = e / jnp.sum(e, axis=0, keepdims=True)
    lb = jnp.sum(sm[0:layer + 1], axis=0, keepdims=True)
    f = lb + (1.0 - lb) * _sigmoid(f_ref[...])
    logf = jnp.log(f)
    kf = 1.0 - f
    pos = lax.broadcasted_iota(jnp.int32, (rows, LANES), 0) % chunk
    cum = logf
    suf = logf
    s = 1
    while s < chunk:
        cum = cum + jnp.where(pos >= s, pltpu.roll(cum, s, axis=0), 0.0)
        suf = suf + jnp.where(pos < chunk - s, pltpu.roll(suf, rows - s, axis=0), 0.0)
        s *= 2
    qin_s[...] = q_ref[...] * jnp.exp(cum)
    kin_s[...] = kf * jnp.exp(-cum)
    kout_s[...] = kf * jnp.exp(suf - logf)
    d_s[...] = jnp.exp(cum + suf - logf)
    so_ref[...] = st_ref[...]

    nch = LANES // chunk
    ri = lax.broadcasted_iota(jnp.int32, (LANES, LANES), 0)
    ci = lax.broadcasted_iota(jnp.int32, (LANES, LANES), 1)
    causal = (ri // chunk == ci // chunk) & (ci <= ri)
    ng = ng_ref[...]

    def tile(ti, carry):
        r0 = pl.multiple_of(ti * LANES, LANES)
        qi = qin_s[pl.ds(r0, LANES), :]
        ki = kin_s[pl.ds(r0, LANES), :]
        ko = kout_s[pl.ds(r0, LANES), :]
        dd = d_s[pl.ds(r0, LANES), :]
        vv = i_ref[pl.ds(r0, LANES), :]
        qb = qi.astype(BF16)
        vb = vv.astype(BF16)
        att = jnp.where(causal, _dot_nt(qb, ki.astype(BF16)), 0.0)
        o = _dot(att.astype(BF16), vb)
        kot = ko.T
        if per_seq:
            sall = jnp.concatenate([so_ref[c] for c in range(nch)], axis=1).astype(BF16)
            big = _dot(qb, sall)
            inter = jnp.zeros((LANES, LANES), F32)
            for c in range(nch):
                inter = inter + jnp.where(ri // chunk == c, big[:, c * LANES:(c + 1) * LANES], 0.0)
            o = o + inter
        parts = []
        for c in range(nch):
            sidx = c if per_seq else 0
            st = so_ref[sidx]
            if not per_seq:
                parts.append(_dot(qb[c * chunk:(c + 1) * chunk], st.astype(BF16)))
            dcol = jnp.broadcast_to(dd[c * chunk:c * chunk + 1, :], (LANES, LANES)).T
            kom = jnp.where(ci // chunk == c, kot, 0.0).astype(BF16)
            so_ref[sidx] = st * dcol + _dot(kom, vb)
        if not per_seq:
            o = o + jnp.concatenate(parts, axis=0)
        o = o * lax.rsqrt(jnp.mean(o * o, axis=-1, keepdims=True) + NORM_EPS) * ng
        gate = g_ref[pl.ds(r0, LANES), :]
        o_ref[pl.ds(r0, LANES), :] = o * (gate * _sigmoid(gate))
        return carry

    lax.fori_loop(0, rows // LANES, tile, 0)


def _hgrn(z2, lb_logits, norm_g, st0, *, b, t, layer):
    chunk = min(HG_CHUNK, t)
    per_seq = t < LANES
    if per_seq:
        rows = LANES
        ns = LANES // t
    else:
        rows = t
        ns = 1
    nblk = (b * t) // rows
    c0 = SEG // LANES

    def col(off):
        return pl.BlockSpec((rows, LANES), lambda g, h: (g, c0 + off * HG_HEADS + h))

    st_spec = pl.BlockSpec((ns, None, HG_DK, HG_DV), lambda g, h: (g, h, 0, 0))
    nd = lb_logits.shape[0]
    scr = pltpu.VMEM((rows, LANES), F32)
    return pl.pallas_call(
        functools.partial(_hgrn_body, chunk=chunk, per_seq=per_seq, layer=layer),
        grid=(nblk, HG_HEADS),
        in_specs=[col(0), col(1), col(2), col(3),
                  pl.BlockSpec((nd, LANES), lambda g, h: (0, h)),
                  pl.BlockSpec((1, LANES), lambda g, h: (0, 0)),
                  st_spec],
        out_specs=[pl.BlockSpec((rows, LANES), lambda g, h: (g, h)), st_spec],
        out_shape=[jax.ShapeDtypeStruct((b * t, HG_HEADS * HG_DV), F32),
                   jax.ShapeDtypeStruct(st0.shape, F32)],
        scratch_shapes=[scr, scr, scr, scr],
        compiler_params=_cparams(("arbitrary", "arbitrary")),
        name="hgrn",
    )(z2, z2, z2, z2, lb_logits, norm_g, st0)


def _merge_body(x_ref, o_ref, bon_ref, g_ref, ob_ref, gate_ref, lnw_ref, lnb_ref,
                wa_ref, wb_ref, wo_ref, nf_ref, wq_ref, keys_ref,
                x1_o, xn_o, st_o):
    o = o_ref[...]
    d = o - _headsum(o) * (1.0 / RW_HEAD_DIM)
    var = _headsum(d * d) * (1.0 / RW_HEAD_DIM)
    oa = d * lax.rsqrt(var + RW_GN_EPS) * lnw_ref[...] + lnb_ref[...]
    oa = (oa + bon_ref[...]) * g_ref[...]
    ya = _dot(oa.astype(BF16), wa_ref[...])
    yb = _dot(ob_ref[...].astype(BF16), wb_ref[...])
    gate = gate_ref[...]
    merged = _sigmoid(gate[:, 0:D_MODEL]) * ya + _sigmoid(gate[:, D_MODEL:]) * yb
    x1 = x_ref[...] + _dot(merged.astype(BF16), wo_ref[...])
    x1_o[...] = x1
    xn = _rms(x1, nf_ref[...]).astype(BF16)
    xn_o[...] = xn
    for c in range(0, PEER_HEADS * PEER_QDIM, 512):
        q = _dot(xn, wq_ref[:, c:c + 512]).astype(BF16)
        for s in range(4):
            hc = c // LANES + s
            st_o[hc] = _dot_nt(keys_ref[hc], q[:, s * LANES:(s + 1) * LANES])


def _merge(x, o_raw, bonus, g, o_b, z2, p):
    n = x.shape[0]
    tm = min(256, n)
    nhc = 2 * PEER_HEADS
    r512 = pl.BlockSpec((tm, RW_WIDTH), lambda i: (i, 0))
    r1024 = pl.BlockSpec((tm, D_MODEL), lambda i: (i, 0))

    def full(a):
        nd = a.ndim
        return pl.BlockSpec(a.shape, lambda i: (0,) * nd)

    ws = [p["ln_w"], p["ln_b"], p["w_up_a"], p["w_up_b"], p["w_out"], p["norm_ffn_g"], p["w_q"], p["keys"]]
    return pl.pallas_call(
        _merge_body,
        grid=(n // tm,),
        in_specs=[r1024, r512, r512, r512, r512,
                  pl.BlockSpec((tm, SEG), lambda i: (i, 2))] + [full(a) for a in ws],
        out_specs=[r1024, r1024, pl.BlockSpec((nhc, PEER_KEYS, tm), lambda i: (0, 0, i))],
        out_shape=[jax.ShapeDtypeStruct((n, D_MODEL), F32),
                   jax.ShapeDtypeStruct((n, D_MODEL), BF16),
                   jax.ShapeDtypeStruct((nhc, PEER_KEYS, n), F32)],
        compiler_params=_cparams(("arbitrary",)),
        name="merge",
    )(x, o_raw, bonus, g, o_b, z2, *ws)


def _top16(x, payloads):
    nrow = x.shape[0]
    rowi = lax.broadcasted_iota(jnp.int32, x.shape, 0).astype(F32)
    slot = lax.broadcasted_iota(jnp.int32, (PEER_TOPK, LANES), 0)
    zero = jnp.zeros((PEER_TOPK, LANES), F32)

    def it(r, carry):
        x, vals, outs = carry
        m = jnp.max(x, axis=0, keepdims=True)
        idx = jnp.min(jnp.where(x == m, rowi, float(nrow)), axis=0, keepdims=True)
        sel = rowi == idx
        here = slot == r
        vals = jnp.where(here, m, vals)
        outs = tuple(jnp.where(here, jnp.sum(jnp.where(sel, pay, 0.0), axis=0, keepdims=True), o)
                     for pay, o in zip(payloads, outs))
        return jnp.where(sel, -jnp.inf, x), vals, outs

    _, vals, outs = lax.fori_loop(0, PEER_TOPK, it, (x, zero, tuple(zero for _ in payloads)))
    return vals, outs


def _topk_body(st_ref, a_o, b_o, g_o, a_s, b_s, g_s):
    rowi = lax.broadcasted_iota(jnp.int32, (PEER_KEYS, LANES), 0).astype(F32)

    def head(h, carry):
        v1, (i1,) = _top16(st_ref[2 * h], (rowi,))
        v2, (i2,) = _top16(st_ref[2 * h + 1], (rowi,))
        cand = jnp.concatenate([v1[k:k + 1] + v2 for k in range(PEER_TOPK)], axis=0)
        pa = jnp.concatenate([jnp.broadcast_to(i1[k:k + 1], (PEER_TOPK, LANES)) for k in range(PEER_TOPK)], axis=0)
        pb = jnp.concatenate([i2] * PEER_TOPK, axis=0)
        cv, (ea, eb) = _top16(cand, (pa, pb))
        ex = jnp.exp(cv - cv[0:1])
        r0 = pl.multiple_of(h * PEER_TOPK, PEER_TOPK)
        a_s[pl.ds(r0, PEER_TOPK), :] = ea
        b_s[pl.ds(r0, PEER_TOPK), :] = eb
        g_s[pl.ds(r0, PEER_TOPK), :] = ex / jnp.sum(ex, axis=0, keepdims=True)
        return carry

    lax.fori_loop(0, PEER_HEADS, head, 0)
    a_o[...] = a_s[...].T
    b_o[...] = b_s[...].T
    g_o[...] = g_s[...].T


def _topk(st):
    nhc, nk, n = st.shape
    out = jax.ShapeDtypeStruct((n, LANES), F32)
    ospec = pl.BlockSpec((LANES, LANES), lambda i: (i, 0))
    scr = pltpu.VMEM((LANES, LANES), F32)
    return pl.pallas_call(
        _topk_body,
        grid=(n // LANES,),
        in_specs=[pl.BlockSpec((nhc, nk, LANES), lambda i: (0, 0, i))],
        out_specs=[ospec] * 3,
        out_shape=[out] * 3,
        scratch_shapes=[scr] * 3,
        compiler_params=_cparams(("arbitrary",)),
        name="topk",
    )(st)


def _peer_body(x1_ref, xn_ref, a_ref, b_ref, gt_ref, u_ref, v_ref, nf_ref, y_ref, gate_s, acc_s, *, tb):
    c = pl.program_id(1)

    @pl.when(c == 0)
    def _():
        acc_s[...] = jnp.zeros_like(acc_s)
        sub = lax.broadcasted_iota(jnp.int32, (PEER_KEYS, LANES), 0).astype(F32)

        def tok(t, carry):
            ar = a_ref[pl.ds(t, 1), :]
            br = b_ref[pl.ds(t, 1), :]
            gr = gt_ref[pl.ds(t, 1), :]
            oa = jnp.where(sub == ar, 1.0, 0.0).astype(BF16)
            ob = jnp.where(sub == br, gr, 0.0).astype(BF16)
            gate_s[pl.ds(pl.multiple_of(t * PEER_KEYS, PEER_KEYS), PEER_KEYS), :] = _dot_nt(oa, ob)
            return carry

        lax.fori_loop(0, tb, tok, 0)

    u = _dot_nt(xn_ref[...], u_ref[...])
    act = 0.5 * u * (1.0 + lax.erf(u * (2.0 ** -0.5)))
    g0 = gate_s[pl.ds(2 * c, tb, stride=PEER_KEYS), :]
    g1 = gate_s[pl.ds(2 * c + 1, tb, stride=PEER_KEYS), :]
    wgt = jnp.concatenate([g0, g1], axis=1) * act
    acc_s[...] += _dot(wgt.astype(BF16), v_ref[...])

    @pl.when(c == pl.num_programs(1) - 1)
    def _():
        y_ref[...] = _rms(x1_ref[...] + acc_s[...], nf_ref[...])


def _peer(x1, xn, a_idx, b_idx, gates, u_tab, v_tab, norm_g):
    n = x1.shape[0]
    tb = min(256, n)
    ne = u_tab.shape[0]
    row = lambda i, c: (i, 0)
    return pl.pallas_call(
        functools.partial(_peer_body, tb=tb),
        grid=(n // tb, ne // PEER_CHUNK),
        in_specs=[pl.BlockSpec((tb, D_MODEL), row),
                  pl.BlockSpec((tb, D_MODEL), row),
                  pl.BlockSpec((tb, LANES), row),
                  pl.BlockSpec((tb, LANES), row),
                  pl.BlockSpec((tb, LANES), row),
                  pl.BlockSpec((PEER_CHUNK, D_MODEL), lambda i, c: (c, 0)),
                  pl.BlockSpec((PEER_CHUNK, D_MODEL), lambda i, c: (c, 0)),
                  pl.BlockSpec((1, D_MODEL), lambda i, c: (0, 0))],
        out_specs=pl.BlockSpec((tb, D_MODEL), row),
        out_shape=jax.ShapeDtypeStruct((n, D_MODEL), F32),
        scratch_shapes=[pltpu.VMEM((tb * PEER_KEYS, LANES), F32),
                        pltpu.VMEM((tb, D_MODEL), F32)],
        compiler_params=_cparams(("arbitrary", "arbitrary")),
        name="peer",
    )(x1, xn, a_idx, b_idx, gates, u_tab, v_tab, norm_g)


def _pad_rows(a, n):
    return jnp.concatenate([a, jnp.zeros((n - a.shape[0],) + a.shape[1:], a.dtype)], axis=0)


def _layer_params(l, w_in, rw_mu, rw_w0, rw_w2, rw_a0, rw_a2, rw_g2, rw_k_k, rw_k_a, rw_r_k, rw_ln_w,
                  rw_ln_b, w_up_a, w_up_b, w_out, norm_ffn_g, peer_w_q, peer_keys):
    hm = jnp.asarray(_HM)
    r0, k0, v0 = 0, RW_WIDTH, 2 * RW_WIDTH
    d0 = 3 * RW_WIDTH
    a0 = d0 + RW_DECAY_RANK
    g0 = a0 + RW_AAA_RANK

    def rw_layout(m):
        z64 = jnp.zeros(m.shape[:-1] + (64,), m.dtype)
        zend = jnp.zeros(m.shape[:-1] + (SEG - RW_PAD,), m.dtype)
        return jnp.concatenate([m[..., r0:k0][..., hm], m[..., k0:v0][..., hm], m[..., v0:d0][..., hm],
                                m[..., d0:a0], z64, m[..., a0:g0], z64, m[..., g0:RW_COLS], zend], axis=-1)

    w = w_in[l]
    w_all = jnp.concatenate([rw_layout(w[:, :RW_COLS]), w[:, RW_COLS:]], axis=1).astype(BF16)
    row = lambda a: a.reshape(1, -1)
    return {
        "w_all": w_all,
        "mu": rw_layout(rw_mu[l]).reshape(1, 1, SEG),
        "w0": row(rw_w0[l][hm]),
        "w2": _pad_rows(rw_w2[l][:, hm], LANES).astype(BF16),
        "a0": row(rw_a0[l][hm]),
        "a2": _pad_rows(rw_a2[l][:, hm], LANES).astype(BF16),
        "g2": rw_g2[l][:, hm].astype(BF16),
        "k_k": row(rw_k_k[l][hm]),
        "k_a": row(rw_k_a[l][hm]),
        "r_k": row(rw_r_k[l].reshape(-1)[hm]),
        "ln_w": row(rw_ln_w[l][hm]),
        "ln_b": row(rw_ln_b[l][hm]),
        "w_up_a": w_up_a[l][hm, :].astype(BF16),
        "w_up_b": w_up_b[l].astype(BF16),
        "w_out": w_out[l].astype(BF16),
        "norm_ffn_g": row(norm_ffn_g[l]),
        "w_q": peer_w_q[l].astype(BF16),
        "keys": peer_keys[l].reshape(2 * PEER_HEADS, PEER_KEYS, PEER_QDIM // 2).astype(BF16),
    }


def _scan_consts():
    lane = np.arange(RW_WIDTH)
    rep = (lane[:, None] % RW_HEADS == np.arange(LANES)[None, :] % RW_HEADS)
    pm = np.zeros((RW_WIDTH, RW_WIDTH), np.float32)
    pm[_HM, lane] = 1.0
    return {"rep": jnp.asarray(rep, BF16), "pm": jnp.asarray(pm, BF16), "pmt": jnp.asarray(pm.T, BF16)}


def _layer(l, x, shift0, wkv0, hg0, p, consts, norm_mix_g, hg_lb_logits, hg_norm_g, u_tab, v_tab, out_norm_g):
    b, t, _ = x.shape
    n = b * t
    x2 = x.reshape(n, D_MODEL)
    gmix = norm_mix_g.reshape(1, D_MODEL)
    z2 = _proj(x2, gmix, p["w_all"], do_norm=True, nseg=3)
    new_shift = _rms_rows(x[:, -1], gmix)
    if shift0 is None:
        zfirst = jnp.zeros((b, 1, SEG), F32)
        s0 = jnp.zeros((b, RW_HEAD_DIM, RW_WIDTH), F32)
        h0 = jnp.zeros((b, HG_HEADS, HG_DK, HG_DV), F32)
    else:
        zfirst = _proj(shift0, gmix, p["w_all"], do_norm=False, nseg=1).reshape(b, 1, SEG)
        s0 = wkv0.transpose(0, 2, 1, 3).reshape(b, RW_HEAD_DIM, RW_WIDTH)
        h0 = hg0
    rows = _rwkv_pre(z2.reshape(b, t, 3 * SEG), zfirst, p)
    nkk, w, bb, k2, r, v, g, bonus = rows
    o_raw, s_fin = _rwkv_scan([nkk, w, bb, k2, r, v], s0, consts)
    new_wkv = s_fin.reshape(b, RW_HEAD_DIM, RW_HEADS, RW_HEAD_DIM).transpose(0, 2, 1, 3)
    o_b, new_hg = _hgrn(z2, hg_lb_logits, hg_norm_g.reshape(1, HG_DV), h0, b=b, t=t, layer=l)
    flat = lambda a: a.reshape(n, RW_WIDTH)
    x1, xn, st = _merge(x2, flat(o_raw), flat(bonus), flat(g), o_b, z2, p)
    a_idx, b_idx, gates = _topk(st)
    y = _peer(x1, xn, a_idx, b_idx, gates, u_tab, v_tab, out_norm_g.reshape(1, D_MODEL))
    return y.reshape(b, t, D_MODEL), new_shift, new_wkv, new_hg


def kernel(x_prompt, x_sample, state_rwkv_shift, state_rwkv_wkv, state_hgrn, norm_mix_g, w_in, rw_mu, rw_w0,
           rw_w2, rw_a0, rw_a2, rw_g2, rw_k_k, rw_k_a, rw_r_k, rw_ln_w, rw_ln_b, hg_lb_logits, hg_norm_g,
           w_up_a, w_up_b, w_out, norm_ffn_g, peer_w_q, peer_keys, peer_u, peer_v, norm_final_g):
    depth = w_in.shape[0]
    assert depth == 1, "the final rmsnorm is fused into the last layer's expert kernel"
    consts = _scan_consts()
    outs = []
    for l in range(depth):
        p = _layer_params(l, w_in, rw_mu, rw_w0, rw_w2, rw_a0, rw_a2, rw_g2, rw_k_k, rw_k_a, rw_r_k,
                          rw_ln_w, rw_ln_b, w_up_a, w_up_b, w_out, norm_ffn_g, peer_w_q, peer_keys)
        u_tab = peer_u[l].astype(BF16)
        v_tab = peer_v[l].astype(BF16)
        common = (p, consts, norm_mix_g[l], hg_lb_logits, hg_norm_g[l], u_tab, v_tab, norm_final_g)
        yp, p_shift, p_wkv, p_hg = _layer(l, x_prompt, None, None, None, *common)
        ys, s_shift, s_wkv, s_hg = _layer(l, x_sample, state_rwkv_shift[l], state_rwkv_wkv[l],
                                          state_hgrn[l], *common)
        outs = (yp, ys, p_shift[None], p_wkv[None], p_hg[None], s_shift[None], s_wkv[None], s_hg[None])
    return outs
```

```python
import functools

import numpy as np
import jax
import jax.numpy as jnp
from jax import lax
from jax.experimental import pallas as pl
from jax.experimental.pallas import tpu as pltpu

F32 = jnp.float32
BF16 = jnp.bfloat16

D_MODEL = 1024
RW_HEADS = 8
RW_HEAD_DIM = 64
RW_WIDTH = 512
RW_DECAY_RANK = 64
RW_AAA_RANK = 64
RW_GATE_RANK = 128
RW_COLS = 1792
RW_GN_EPS = 64e-5
HG_HEADS = 4
HG_DK = 128
HG_DV = 128
HG_COLS = 2048
HG_CHUNK = 32
GATE_COLS = 2048
PEER_KEYS = 128
PEER_HEADS = 8
PEER_TOPK = 16
PEER_QDIM = 256
NORM_EPS = 1e-6

LANES = 128
SEG = 2048
RW_PAD = 1920
SCAN_NB = 8
PEER_CHUNK = 256
PEER_STEP = 1024
PEER_TOKENS = 512
VMEM_LIMIT = 60000 * 1024

_HM = (np.arange(RW_WIDTH) % RW_HEADS) * RW_HEAD_DIM + np.arange(RW_WIDTH) // RW_HEADS


def _cparams(sem):
    return pltpu.CompilerParams(dimension_semantics=sem, vmem_limit_bytes=VMEM_LIMIT)


def _sigmoid(x):
    return 1.0 / (1.0 + jnp.exp(-x))


def _rms(x, g):
    return x * lax.rsqrt(jnp.mean(x * x, axis=-1, keepdims=True) + NORM_EPS) * g


def _headsum(x):
    s = x[:, 0:128] + x[:, 128:256] + x[:, 256:384] + x[:, 384:512]
    for sh in (64, 32, 16, 8):
        s = s + pltpu.roll(s, sh, axis=1)
    return jnp.concatenate([s, s, s, s], axis=1)


def _dot(a, b):
    return jnp.dot(a, b, preferred_element_type=F32)


def _dot_nt(a, b):
    return lax.dot_general(a, b, (((1,), (1,)), ((), ())), preferred_element_type=F32)


def _exact_perm(x, p):
    x1 = x.astype(BF16)
    r1 = x - x1.astype(F32)
    x2 = r1.astype(BF16)
    x3 = (r1 - x2.astype(F32)).astype(BF16)
    return _dot(x1, p) + _dot(x2, p) + _dot(x3, p)


def _proj_body(x_ref, g_ref, w_ref, o_ref, *, do_norm):
    x = x_ref[...]
    if do_norm:
        x = _rms(x, g_ref[...])
    xb = x.astype(BF16)
    for c in range(0, SEG, 512):
        o_ref[:, c:c + 512] = _dot(xb, w_ref[:, c:c + 512])


def _proj(x, g, w, *, do_norm, nseg):
    n = x.shape[0]
    tm = min(512, n)
    return pl.pallas_call(
        functools.partial(_proj_body, do_norm=do_norm),
        grid=(nseg, n // tm),
        in_specs=[pl.BlockSpec((tm, D_MODEL), lambda c, i: (i, 0)),
                  pl.BlockSpec((1, D_MODEL), lambda c, i: (0, 0)),
                  pl.BlockSpec((D_MODEL, SEG), lambda c, i: (0, c))],
        out_specs=pl.BlockSpec((tm, SEG), lambda c, i: (i, c)),
        out_shape=jax.ShapeDtypeStruct((n, nseg * SEG), F32),
        compiler_params=_cparams(("arbitrary", "arbitrary")),
        name="proj",
    )(x, g, w)


def _rms_rows_body(x_ref, g_ref, o_ref):
    o_ref[...] = _rms(x_ref[...], g_ref[...])


def _rms_rows(x, g):
    return pl.pallas_call(
        _rms_rows_body,
        out_shape=jax.ShapeDtypeStruct(x.shape, F32),
        name="rms_rows",
    )(x, g)


def _rwkv_pre_body(z_ref, zp_ref, f_ref, mu_ref, w0_ref, w2_ref, a0_ref, a2_ref, g2_ref,
                   kk_ref, ka_ref, rk_ref,
                   nkk_o, w_o, bb_o, k2_o, r_o, v_o, g_o, bon_o):
    j = pl.program_id(1)
    z = z_ref[...]
    bt, tt, c = z.shape
    pos = lax.broadcasted_iota(jnp.int32, z.shape, 1)
    first = jnp.where(j == 0, f_ref[...], zp_ref[:, 7:8, :])
    zprev = jnp.where(pos == 0, first, pltpu.roll(z, 1, axis=1))
    zs = (z + (zprev - z) * mu_ref[...]).reshape(bt * tt, c)
    r = zs[:, 0:512]
    k = zs[:, 512:1024]
    v = zs[:, 1024:1536]
    dl = zs[:, 1536:1664]
    al = zs[:, 1664:1792]
    gl = zs[:, 1792:1920]
    y = w0_ref[...] + _dot(jnp.tanh(dl).astype(BF16), w2_ref[...])
    softplus_neg = jnp.maximum(-y, 0.0) + jnp.log(1.0 + jnp.exp(-jnp.abs(y)))
    decay = jnp.exp(-jnp.exp(-softplus_neg - 0.5))
    a = _sigmoid(a0_ref[...] + _dot(al.astype(BF16), a2_ref[...]))
    g = _dot(_sigmoid(gl).astype(BF16), g2_ref[...])
    kk = k * kk_ref[...]
    kk = kk / jnp.maximum(jnp.sqrt(_headsum(kk * kk)), 1e-12)
    k2 = k * (1.0 + (a - 1.0) * ka_ref[...])
    bonus = _headsum(r * k2 * rk_ref[...]) * v
    shp = (bt, tt, RW_WIDTH)
    nkk_o[...] = (-kk).reshape(shp)
    w_o[...] = decay.reshape(shp)
    bb_o[...] = (kk * a).reshape(shp)
    k2_o[...] = k2.reshape(shp)
    r_o[...] = r.reshape(shp)
    v_o[...] = v.reshape(shp)
    g_o[...] = g.reshape(shp)
    bon_o[...] = bonus.reshape(shp)


def _rwkv_pre(z3, zfirst, p):
    b, t, _ = z3.shape
    if t >= 256:
        bt, tt = 1, 256
    else:
        bt, tt = min(b, 256 // t), t
    tpb = tt // 8
    row = lambda bi, j: (bi, j, 0)
    const2 = lambda bi, j: (0, 0)
    vec = pl.BlockSpec((1, RW_WIDTH), const2)
    low = pl.BlockSpec((LANES, RW_WIDTH), const2)
    out_spec = pl.BlockSpec((bt, tt, RW_WIDTH), row)
    out_shape = jax.ShapeDtypeStruct((b, t, RW_WIDTH), F32)
    return pl.pallas_call(
        _rwkv_pre_body,
        grid=(b // bt, t // tt),
        in_specs=[pl.BlockSpec((bt, tt, SEG), row),
                  pl.BlockSpec((bt, 8, SEG), lambda bi, j: (bi, jnp.maximum(j * tpb - 1, 0), 0)),
                  pl.BlockSpec((bt, 1, SEG), lambda bi, j: (bi, 0, 0)),
                  pl.BlockSpec((1, 1, SEG), lambda bi, j: (0, 0, 0)),
                  vec, low, vec, low, low, vec, vec, vec],
        out_specs=[out_spec] * 8,
        out_shape=[out_shape] * 8,
        compiler_params=_cparams(("arbitrary", "arbitrary")),
        name="rwkv_pre",
    )(z3, z3, zfirst, p["mu"], p["w0"], p["w2"], p["a0"], p["a2"], p["g2"],
      p["k_k"], p["k_a"], p["r_k"])


def _rwkv_scan_body(nkk_ref, w_ref, bb_ref, k2_ref, r_ref, v_ref, s0_ref, rep_ref, pm_ref, pmt_ref,
                    o_ref, sf_ref, s_scr, *, tc):
    j = pl.program_id(1)

    @pl.when(j == 0)
    def _():
        for b in range(SCAN_NB):
            s_scr[b] = _exact_perm(s0_ref[b], pm_ref[...])

    shape = (RW_HEAD_DIM, RW_WIDTH)
    diag = (lax.broadcasted_iota(jnp.int32, shape, 1) >> 3) == lax.broadcasted_iota(jnp.int32, shape, 0)

    def tile4(x):
        return jnp.concatenate([x, x, x, x], axis=1)

    def lanesum(x):
        s = x[:, 0:128] + x[:, 128:256] + x[:, 256:384] + x[:, 384:512]
        for sh in (64, 32, 16, 8):
            s = s + pltpu.roll(s, sh, axis=1)
        return s

    def step(t, carry):
        for b in range(SCAN_NB):
            nkk = nkk_ref[b, pl.ds(t, 1), :]
            wt = w_ref[b, pl.ds(t, 1), :]
            bbt = bb_ref[b, pl.ds(t, 1), :]
            k2t = k2_ref[b, pl.ds(t, 1), :]
            rt = r_ref[b, pl.ds(t, 1), :]
            vt = v_ref[b, pl.ds(t, 1), :]
            s = s_scr[b]
            sa = tile4(lanesum(s * nkk))
            vcol = tile4(_dot(jnp.where(diag, vt, 0.0).astype(BF16), rep_ref[...]))
            s = s * wt + sa * bbt + vcol * k2t
            s_scr[b] = s
            oc = tile4(lanesum(s * rt))
            o_ref[b, pl.ds(t, 1), :] = jnp.sum(jnp.where(diag, oc, 0.0), axis=0, keepdims=True)
        return carry

    lax.fori_loop(0, tc, step, 0)

    @pl.when(j == pl.num_programs(1) - 1)
    def _():
        for b in range(SCAN_NB):
            sf_ref[b] = _exact_perm(s_scr[b], pmt_ref[...])


def _rwkv_scan(rows, s0, consts):
    b, t, _ = rows[0].shape
    tc = min(64, t)
    row = pl.BlockSpec((SCAN_NB, tc, RW_WIDTH), lambda g, j: (g, j, 0))
    st = pl.BlockSpec((SCAN_NB, RW_HEAD_DIM, RW_WIDTH), lambda g, j: (g, 0, 0))
    const2 = lambda g, j: (0, 0)
    return pl.pallas_call(
        functools.partial(_rwkv_scan_body, tc=tc),
        grid=(b // SCAN_NB, t // tc),
        in_specs=[row] * 6 + [st,
                              pl.BlockSpec((RW_WIDTH, LANES), const2),
                              pl.BlockSpec((RW_WIDTH, RW_WIDTH), const2),
                              pl.BlockSpec((RW_WIDTH, RW_WIDTH), const2)],
        out_specs=[row, st],
        out_shape=[jax.ShapeDtypeStruct((b, t, RW_WIDTH), F32),
                   jax.ShapeDtypeStruct((b, RW_HEAD_DIM, RW_WIDTH), F32)],
        scratch_shapes=[pltpu.VMEM((SCAN_NB, RW_HEAD_DIM, RW_WIDTH), F32)],
        compiler_params=_cparams(("arbitrary", "arbitrary")),
        name="rwkv_scan",
    )(*rows, s0, consts["rep"], consts["pm"], consts["pmt"])


def _hgrn_body(q_ref, f_ref, i_ref, g_ref, lbl_ref, ng_ref, st_ref, o_ref, so_ref,
               qin_s, kin_s, kout_s, d_s, *, chunk, per_seq, layer):
    rows = q_ref.shape[0]
    lg = lbl_ref[...]
    e = jnp.exp(lg - jnp.max(lg, axis=0, keepdims=True))
    sm = e / jnp.sum(e, axis=0, keepdims=True)
    lb = jnp.sum(sm[0:layer + 1], axis=0, keepdims=True)
    f = lb + (1.0 - lb) * _sigmoid(f_ref[...])
    logf = jnp.log(f)
    kf = 1.0 - f
    pos = lax.broadcasted_iota(jnp.int32, (rows, LANES), 0) % chunk
    cum = logf
    suf = logf
    s = 1
    while s < chunk:
        cum = cum + jnp.where(pos >= s, pltpu.roll(cum, s, axis=0), 0.0)
        suf = suf + jnp.where(pos < chunk - s, pltpu.roll(suf, rows - s, axis=0), 0.0)
        s *= 2
    qin_s[...] = q_ref[...] * jnp.exp(cum)
    kin_s[...] = kf * jnp.exp(-cum)
    kout_s[...] = kf * jnp.exp(suf - logf)
    d_s[...] = jnp.exp(cum + suf - logf)
    so_ref[...] = st_ref[...]

    nch = LANES // chunk
    ri = lax.broadcasted_iota(jnp.int32, (LANES, LANES), 0)
    ci = lax.broadcasted_iota(jnp.int32, (LANES, LANES), 1)
    causal = (ri // chunk == ci // chunk) & (ci <= ri)
    ng = ng_ref[...]

    def tile(ti, carry):
        r0 = pl.multiple_of(ti * LANES, LANES)
        qi = qin_s[pl.ds(r0, LANES), :]
        ki = kin_s[pl.ds(r0, LANES), :]
        ko = kout_s[pl.ds(r0, LANES), :]
        dd = d_s[pl.ds(r0, LANES), :]
        vv = i_ref[pl.ds(r0, LANES), :]
        qb = qi.astype(BF16)
        vb = vv.astype(BF16)
        att = jnp.where(causal, _dot_nt(qb, ki.astype(BF16)), 0.0)
        o = _dot(att.astype(BF16), vb)
        kot = ko.T
        if per_seq:
            sall = jnp.concatenate([so_ref[c] for c in range(nch)], axis=1).astype(BF16)
            big = _dot(qb, sall)
            inter = jnp.zeros((LANES, LANES), F32)
            for c in range(nch):
                inter = inter + jnp.where(ri // chunk == c, big[:, c * LANES:(c + 1) * LANES], 0.0)
            o = o + inter
        parts = []
        for c in range(nch):
            sidx = c if per_seq else 0
            st = so_ref[sidx]
            if not per_seq:
                parts.append(_dot(qb[c * chunk:(c + 1) * chunk], st.astype(BF16)))
            dcol = jnp.broadcast_to(dd[c * chunk:c * chunk + 1, :], (LANES, LANES)).T
            kom = jnp.where(ci // chunk == c, kot, 0.0).astype(BF16)
            so_ref[sidx] = st * dcol + _dot(kom, vb)
        if not per_seq:
            o = o + jnp.concatenate(parts, axis=0)
        o = o * lax.rsqrt(jnp.mean(o * o, axis=-1, keepdims=True) + NORM_EPS) * ng
        gate = g_ref[pl.ds(r0, LANES), :]
        o_ref[pl.ds(r0, LANES), :] = o * (gate * _sigmoid(gate))
        return carry

    lax.fori_loop(0, rows // LANES, tile, 0)


def _hgrn(z2, lb_logits, norm_g, st0, *, b, t, layer):
    chunk = min(HG_CHUNK, t)
    per_seq = t < LANES
    if per_seq:
        rows = LANES
        ns = LANES // t
    else:
        rows = t
        ns = 1
    nblk = (b * t) // rows
    c0 = SEG // LANES

    def col(off):
        return pl.BlockSpec((rows, LANES), lambda g, h: (g, c0 + off * HG_HEADS + h))

    st_spec = pl.BlockSpec((ns, None, HG_DK, HG_DV), lambda g, h: (g, h, 0, 0))
    nd = lb_logits.shape[0]
    scr = pltpu.VMEM((rows, LANES), F32)
    return pl.pallas_call(
        functools.partial(_hgrn_body, chunk=chunk, per_seq=per_seq, layer=layer),
        grid=(nblk, HG_HEADS),
        in_specs=[col(0), col(1), col(2), col(3),
                  pl.BlockSpec((nd, LANES), lambda g, h: (0, h)),
                  pl.BlockSpec((1, LANES), lambda g, h: (0, 0)),
                  st_spec],
        out_specs=[pl.BlockSpec((rows, LANES), lambda g, h: (g, h)), st_spec],
        out_shape=[jax.ShapeDtypeStruct((b * t, HG_HEADS * HG_DV), F32),
                   jax.ShapeDtypeStruct(st0.shape, F32)],
        scratch_shapes=[scr, scr, scr, scr],
        compiler_params=_cparams(("arbitrary", "arbitrary")),
        name="hgrn",
    )(z2, z2, z2, z2, lb_logits, norm_g, st0)


def _merge_body(x_ref, o_ref, bon_ref, g_ref, ob_ref, gate_ref, lnw_ref, lnb_ref,
                wa_ref, wb_ref, wo_ref, nf_ref, wq_ref, keys_ref,
                x1_o, xn_o, st_o):
    o = o_ref[...]
    d = o - _headsum(o) * (1.0 / RW_HEAD_DIM)
    var = _headsum(d * d) * (1.0 / RW_HEAD_DIM)
    oa = d * lax.rsqrt(var + RW_GN_EPS) * lnw_ref[...] + lnb_ref[...]
    oa = (oa + bon_ref[...]) * g_ref[...]
    ya = _dot(oa.astype(BF16), wa_ref[...])
    yb = _dot(ob_ref[...].astype(BF16), wb_ref[...])
    gate = gate_ref[...]
    merged = _sigmoid(gate[:, 0:D_MODEL]) * ya + _sigmoid(gate[:, D_MODEL:]) * yb
    x1 = x_ref[...] + _dot(merged.astype(BF16), wo_ref[...])
    x1_o[...] = x1
    xn = _rms(x1, nf_ref[...]).astype(BF16)
    xn_o[...] = xn
    for c in range(0, PEER_HEADS * PEER_QDIM, 512):
        q = _dot(xn, wq_ref[:, c:c + 512]).astype(BF16)
        for s in range(4):
            hc = c // LANES + s
            st_o[hc] = _dot_nt(keys_ref[hc], q[:, s * LANES:(s + 1) * LANES])


def _merge(x, o_raw, bonus, g, o_b, z2, p):
    n = x.shape[0]
    tm = min(256, n)
    nhc = 2 * PEER_HEADS
    r512 = pl.BlockSpec((tm, RW_WIDTH), lambda i: (i, 0))
    r1024 = pl.BlockSpec((tm, D_MODEL), lambda i: (i, 0))

    def full(a):
        nd = a.ndim
        return pl.BlockSpec(a.shape, lambda i: (0,) * nd)

    ws = [p["ln_w"], p["ln_b"], p["w_up_a"], p["w_up_b"], p["w_out"], p["norm_ffn_g"], p["w_q"], p["keys"]]
    return pl.pallas_call(
        _merge_body,
        grid=(n // tm,),
        in_specs=[r1024, r512, r512, r512, r512,
                  pl.BlockSpec((tm, SEG), lambda i: (i, 2))] + [full(a) for a in ws],
        out_specs=[r1024, r1024, pl.BlockSpec((nhc, PEER_KEYS, tm), lambda i: (0, 0, i))],
        out_shape=[jax.ShapeDtypeStruct((n, D_MODEL), F32),
                   jax.ShapeDtypeStruct((n, D_MODEL), BF16),
                   jax.ShapeDtypeStruct((nhc, PEER_KEYS, n), F32)],
        compiler_params=_cparams(("arbitrary",)),
        name="merge",
    )(x, o_raw, bonus, g, o_b, z2, *ws)


def _top16(xs, payloads):
    nrow = xs[0].shape[0]
    rowi = lax.broadcasted_iota(jnp.int32, xs[0].shape, 0).astype(F32)
    slot = lax.broadcasted_iota(jnp.int32, (PEER_TOPK, LANES), 0)
    zero = jnp.zeros((PEER_TOPK, LANES), F32)

    def it(r, carry):
        here = slot == r
        new = []
        for (x, vals, outs), pay in zip(carry, payloads):
            m = jnp.max(x, axis=0, keepdims=True)
            idx = jnp.min(jnp.where(x == m, rowi, float(nrow)), axis=0, keepdims=True)
            sel = rowi == idx
            vals = jnp.where(here, m, vals)
            if pay is None:
                outs = (jnp.where(here, idx, outs[0]),)
            else:
                outs = tuple(jnp.where(here, jnp.sum(jnp.where(sel, p, 0.0), axis=0, keepdims=True), o)
                             for p, o in zip(pay, outs))
            new.append((jnp.where(sel, -jnp.inf, x), vals, outs))
        return tuple(new)

    init = tuple((x, zero, tuple(zero for _ in (pay or (None,)))) for x, pay in zip(xs, payloads))
    return [(vals, outs) for _, vals, outs in lax.fori_loop(0, PEER_TOPK, it, init)]


def _pair_candidates(v1, i1, v2, i2):
    sub = 8
    row8 = lax.broadcasted_iota(jnp.int32, (sub, LANES), 0)
    cand = [v1[0:1] + v2]
    pa = [jnp.broadcast_to(i1[0:1], (PEER_TOPK, LANES))]
    pb = [i2]
    for k1 in range(1, sub):
        nvalid = PEER_TOPK // (k1 + 1)
        cand.append(jnp.where(row8 < nvalid, v1[k1:k1 + 1] + v2[0:sub], -jnp.inf))
        pa.append(jnp.broadcast_to(i1[k1:k1 + 1], (sub, LANES)))
        pb.append(i2[0:sub])
    cand.append(v1[sub:] + v2[0:1])
    pa.append(i1[sub:])
    pb.append(jnp.broadcast_to(i2[0:1], (PEER_TOPK - sub, LANES)))
    return jnp.concatenate(cand, axis=0), jnp.concatenate(pa, axis=0), jnp.concatenate(pb, axis=0)


TOPK_HEADS_PER_ROUND = 4


def _topk_body(st_ref, a_o, b_o, g_o, a_s, b_s, g_s):
    nh = TOPK_HEADS_PER_ROUND

    def group(gi, carry):
        halves = _top16([st_ref[2 * nh * gi + i] for i in range(2 * nh)], [None] * (2 * nh))
        cands = [_pair_candidates(halves[2 * i][0], halves[2 * i][1][0], halves[2 * i + 1][0], halves[2 * i + 1][1][0])
                 for i in range(nh)]
        picks = _top16([c[0] for c in cands], [(c[1], c[2]) for c in cands])
        for i, (cv, (ea, eb)) in enumerate(picks):
            ex = jnp.exp(cv - cv[0:1])
            r0 = pl.multiple_of((gi * nh + i) * PEER_TOPK, PEER_TOPK)
            a_s[pl.ds(r0, PEER_TOPK), :] = ea
            b_s[pl.ds(r0, PEER_TOPK), :] = eb
            g_s[pl.ds(r0, PEER_TOPK), :] = ex / jnp.sum(ex, axis=0, keepdims=True)
        return carry

    lax.fori_loop(0, PEER_HEADS // nh, group, 0)
    a_o[...] = a_s[...].T
    b_o[...] = b_s[...].T
    g_o[...] = g_s[...].T


def _topk(st):
    nhc, nk, n = st.shape
    out = jax.ShapeDtypeStruct((n, LANES), F32)
    ospec = pl.BlockSpec((LANES, LANES), lambda i: (i, 0))
    scr = pltpu.VMEM((LANES, LANES), F32)
    return pl.pallas_call(
        _topk_body,
        grid=(n // LANES,),
        in_specs=[pl.BlockSpec((nhc, nk, LANES), lambda i: (0, 0, i))],
        out_specs=[ospec] * 3,
        out_shape=[out] * 3,
        scratch_shapes=[scr] * 3,
        compiler_params=_cparams(("arbitrary",)),
        name="topk",
    )(st)


def _peer_body(x1_ref, xn_ref, a_ref, b_ref, gt_ref, u_ref, v_ref, nf_ref, y_ref, gate_s, acc_s, *, tb):
    c = pl.program_id(1)

    @pl.when(c == 0)
    def _():
        acc_s[...] = jnp.zeros_like(acc_s)
        sub = lax.broadcasted_iota(jnp.int32, (PEER_KEYS, LANES), 0).astype(F32)

        def tok(t, carry):
            ar = a_ref[pl.ds(t, 1), :]
            br = b_ref[pl.ds(t, 1), :]
            gr = gt_ref[pl.ds(t, 1), :]
            oa = jnp.where(sub == ar, 1.0, 0.0).astype(BF16)
            ob = jnp.where(sub == br, gr, 0.0).astype(BF16)
            gate_s[pl.ds(pl.multiple_of(t * PEER_KEYS, PEER_KEYS), PEER_KEYS), :] = _dot_nt(oa, ob)
            return carry

        lax.fori_loop(0, tb, tok, 0, unroll=32)

    xn = xn_ref[...]
    parts = []
    for k in range(PEER_STEP // PEER_CHUNK):
        u = _dot_nt(xn, u_ref[k * PEER_CHUNK:(k + 1) * PEER_CHUNK, :])
        act = 0.5 * u * (1.0 + lax.erf(u * (2.0 ** -0.5)))
        row = c * (PEER_STEP // PEER_KEYS) + 2 * k
        g0 = gate_s[pl.ds(row, tb, stride=PEER_KEYS), :]
        g1 = gate_s[pl.ds(row + 1, tb, stride=PEER_KEYS), :]
        parts.append((jnp.concatenate([g0, g1], axis=1) * act).astype(BF16))
    acc_s[...] += _dot(jnp.concatenate(parts, axis=1), v_ref[...])

    @pl.when(c == pl.num_programs(1) - 1)
    def _():
        y_ref[...] = _rms(x1_ref[...] + acc_s[...], nf_ref[...])


def _peer(x1, xn, a_idx, b_idx, gates, u_tab, v_tab, norm_g):
    n = x1.shape[0]
    tb = min(PEER_TOKENS, n)
    ne = u_tab.shape[0]
    row = lambda i, c: (i, 0)
    return pl.pallas_call(
        functools.partial(_peer_body, tb=tb),
        grid=(n // tb, ne // PEER_STEP),
        in_specs=[pl.BlockSpec((tb, D_MODEL), row),
                  pl.BlockSpec((tb, D_MODEL), row),
                  pl.BlockSpec((tb, LANES), row),
                  pl.BlockSpec((tb, LANES), row),
                  pl.BlockSpec((tb, LANES), row),
                  pl.BlockSpec((PEER_STEP, D_MODEL), lambda i, c: (c, 0)),
                  pl.BlockSpec((PEER_STEP, D_MODEL), lambda i, c: (c, 0)),
                  pl.BlockSpec((1, D_MODEL), lambda i, c: (0, 0))],
        out_specs=pl.BlockSpec((tb, D_MODEL), row),
        out_shape=jax.ShapeDtypeStruct((n, D_MODEL), F32),
        scratch_shapes=[pltpu.VMEM((tb * PEER_KEYS, LANES), F32),
                        pltpu.VMEM((tb, D_MODEL), F32)],
        compiler_params=_cparams(("arbitrary", "arbitrary")),
        name="peer",
    )(x1, xn, a_idx, b_idx, gates, u_tab, v_tab, norm_g)


def _pad_rows(a, n):
    return jnp.concatenate([a, jnp.zeros((n - a.shape[0],) + a.shape[1:], a.dtype)], axis=0)


def _layer_params(l, w_in, rw_mu, rw_w0, rw_w2, rw_a0, rw_a2, rw_g2, rw_k_k, rw_k_a, rw_r_k, rw_ln_w,
                  rw_ln_b, w_up_a, w_up_b, w_out, norm_ffn_g, peer_w_q, peer_keys):
    hm = jnp.asarray(_HM)
    r0, k0, v0 = 0, RW_WIDTH, 2 * RW_WIDTH
    d0 = 3 * RW_WIDTH
    a0 = d0 + RW_DECAY_RANK
    g0 = a0 + RW_AAA_RANK

    def rw_layout(m):
        z64 = jnp.zeros(m.shape[:-1] + (64,), m.dtype)
        zend = jnp.zeros(m.shape[:-1] + (SEG - RW_PAD,), m.dtype)
        return jnp.concatenate([m[..., r0:k0][..., hm], m[..., k0:v0][..., hm], m[..., v0:d0][..., hm],
                                m[..., d0:a0], z64, m[..., a0:g0], z64, m[..., g0:RW_COLS], zend], axis=-1)

    w = w_in[l]
    w_all = jnp.concatenate([rw_layout(w[:, :RW_COLS]), w[:, RW_COLS:]], axis=1).astype(BF16)
    row = lambda a: a.reshape(1, -1)
    return {
        "w_all": w_all,
        "mu": rw_layout(rw_mu[l]).reshape(1, 1, SEG),
        "w0": row(rw_w0[l][hm]),
        "w2": _pad_rows(rw_w2[l][:, hm], LANES).astype(BF16),
        "a0": row(rw_a0[l][hm]),
        "a2": _pad_rows(rw_a2[l][:, hm], LANES).astype(BF16),
        "g2": rw_g2[l][:, hm].astype(BF16),
        "k_k": row(rw_k_k[l][hm]),
        "k_a": row(rw_k_a[l][hm]),
        "r_k": row(rw_r_k[l].reshape(-1)[hm]),
        "ln_w": row(rw_ln_w[l][hm]),
        "ln_b": row(rw_ln_b[l][hm]),
        "w_up_a": w_up_a[l][hm, :].astype(BF16),
        "w_up_b": w_up_b[l].astype(BF16),
        "w_out": w_out[l].astype(BF16),
        "norm_ffn_g": row(norm_ffn_g[l]),
        "w_q": peer_w_q[l].astype(BF16),
        "keys": peer_keys[l].reshape(2 * PEER_HEADS, PEER_KEYS, PEER_QDIM // 2).astype(BF16),
    }


def _scan_consts():
    lane = np.arange(RW_WIDTH)
    rep = (lane[:, None] % RW_HEADS == np.arange(LANES)[None, :] % RW_HEADS)
    pm = np.zeros((RW_WIDTH, RW_WIDTH), np.float32)
    pm[_HM, lane] = 1.0
    return {"rep": jnp.asarray(rep, BF16), "pm": jnp.asarray(pm, BF16), "pmt": jnp.asarray(pm.T, BF16)}


def _layer(l, x, shift0, wkv0, hg0, p, consts, norm_mix_g, hg_lb_logits, hg_norm_g, u_tab, v_tab, out_norm_g):
    b, t, _ = x.shape
    n = b * t
    x2 = x.reshape(n, D_MODEL)
    gmix = norm_mix_g.reshape(1, D_MODEL)
    z2 = _proj(x2, gmix, p["w_all"], do_norm=True, nseg=3)
    new_shift = _rms_rows(x[:, -1], gmix)
    if shift0 is None:
        zfirst = jnp.zeros((b, 1, SEG), F32)
        s0 = jnp.zeros((b, RW_HEAD_DIM, RW_WIDTH), F32)
        h0 = jnp.zeros((b, HG_HEADS, HG_DK, HG_DV), F32)
    else:
        zfirst = _proj(shift0, gmix, p["w_all"], do_norm=False, nseg=1).reshape(b, 1, SEG)
        s0 = wkv0.transpose(0, 2, 1, 3).reshape(b, RW_HEAD_DIM, RW_WIDTH)
        h0 = hg0
    rows = _rwkv_pre(z2.reshape(b, t, 3 * SEG), zfirst, p)
    nkk, w, bb, k2, r, v, g, bonus = rows
    o_raw, s_fin = _rwkv_scan([nkk, w, bb, k2, r, v], s0, consts)
    new_wkv = s_fin.reshape(b, RW_HEAD_DIM, RW_HEADS, RW_HEAD_DIM).transpose(0, 2, 1, 3)
    o_b, new_hg = _hgrn(z2, hg_lb_logits, hg_norm_g.reshape(1, HG_DV), h0, b=b, t=t, layer=l)
    flat = lambda a: a.reshape(n, RW_WIDTH)
    x1, xn, st = _merge(x2, flat(o_raw), flat(bonus), flat(g), o_b, z2, p)
    a_idx, b_idx, gates = _topk(st)
    y = _peer(x1, xn, a_idx, b_idx, gates, u_tab, v_tab, out_norm_g.reshape(1, D_MODEL))
    return y.reshape(b, t, D_MODEL), new_shift, new_wkv, new_hg


def kernel(x_prompt, x_sample, state_rwkv_shift, state_rwkv_wkv, state_hgrn, norm_mix_g, w_in, rw_mu, rw_w0,
           rw_w2, rw_a0, rw_a2, rw_g2, rw_k_k, rw_k_a, rw_r_k, rw_ln_w, rw_ln_b, hg_lb_logits, hg_norm_g,
           w_up_a, w_up_b, w_out, norm_ffn_g, peer_w_q, peer_keys, peer_u, peer_v, norm_final_g):
    depth = w_in.shape[0]
    assert depth == 1, "the final rmsnorm is fused into the last layer's expert kernel"
    consts = _scan_consts()
    outs = []
    for l in range(depth):
        p = _layer_params(l, w_in, rw_mu, rw_w0, rw_w2, rw_a0, rw_a2, rw_g2, rw_k_k, rw_k_a, rw_r_k,
                          rw_ln_w, rw_ln_b, w_up_a, w_up_b, w_out, norm_ffn_g, peer_w_q, peer_keys)
        u_tab = peer_u[l].astype(BF16)
        v_tab = peer_v[l].astype(BF16)
        common = (p, consts, norm_mix_g[l], hg_lb_logits, hg_norm_g[l], u_tab, v_tab, norm_final_g)
        yp, p_shift, p_wkv, p_hg = _layer(l, x_prompt, None, None, None, *common)
        ys, s_shift, s_wkv, s_hg = _layer(l, x_sample, state_rwkv_shift[l], state_rwkv_wkv[l],
                                          state_hgrn[l], *common)
        outs = (yp, ys, p_shift[None], p_wkv[None], p_hg[None], s_shift[None], s_wkv[None], s_hg[None])
    return outs
```

```python
import functools

import numpy as np
import jax
import jax.numpy as jnp
from jax import lax
from jax.experimental import pallas as pl
from jax.experimental.pallas import tpu as pltpu

F32 = jnp.float32
BF16 = jnp.bfloat16

D_MODEL = 1024
RW_HEADS = 8
RW_HEAD_DIM = 64
RW_WIDTH = 512
RW_DECAY_RANK = 64
RW_AAA_RANK = 64
RW_GATE_RANK = 128
RW_COLS = 1792
RW_GN_EPS = 64e-5
HG_HEADS = 4
HG_DK = 128
HG_DV = 128
HG_COLS = 2048
HG_CHUNK = 32
GATE_COLS = 2048
PEER_KEYS = 128
PEER_HEADS = 8
PEER_TOPK = 16
PEER_QDIM = 256
NORM_EPS = 1e-6

LANES = 128
SEG = 2048
RW_PAD = 1920
SCAN_NB = 8
SCAN_KH = RW_HEAD_DIM // (LANES // (RW_HEADS * SCAN_NB))
PEER_CHUNK = 256
PEER_STEP = 1024
PEER_TOKENS = 512
VMEM_LIMIT = 60000 * 1024

_HM = (np.arange(RW_WIDTH) % RW_HEADS) * RW_HEAD_DIM + np.arange(RW_WIDTH) // RW_HEADS


def _cparams(sem):
    return pltpu.CompilerParams(dimension_semantics=sem, vmem_limit_bytes=VMEM_LIMIT)


def _sigmoid(x):
    return 1.0 / (1.0 + jnp.exp(-x))


def _rms(x, g):
    return x * lax.rsqrt(jnp.mean(x * x, axis=-1, keepdims=True) + NORM_EPS) * g


def _headsum(x):
    s = x[:, 0:128] + x[:, 128:256] + x[:, 256:384] + x[:, 384:512]
    for sh in (64, 32, 16, 8):
        s = s + pltpu.roll(s, sh, axis=1)
    return jnp.concatenate([s, s, s, s], axis=1)


def _dot(a, b):
    return jnp.dot(a, b, preferred_element_type=F32)


def _dot_nt(a, b):
    return lax.dot_general(a, b, (((1,), (1,)), ((), ())), preferred_element_type=F32)


def _proj_body(x_ref, g_ref, w_ref, o_ref, *, do_norm):
    x = x_ref[...]
    if do_norm:
        x = _rms(x, g_ref[...])
    xb = x.astype(BF16)
    for c in range(0, SEG, 512):
        o_ref[:, c:c + 512] = _dot(xb, w_ref[:, c:c + 512])


def _proj(x, g, w, *, do_norm, nseg):
    n = x.shape[0]
    tm = min(512, n)
    return pl.pallas_call(
        functools.partial(_proj_body, do_norm=do_norm),
        grid=(nseg, n // tm),
        in_specs=[pl.BlockSpec((tm, D_MODEL), lambda c, i: (i, 0)),
                  pl.BlockSpec((1, D_MODEL), lambda c, i: (0, 0)),
                  pl.BlockSpec((D_MODEL, SEG), lambda c, i: (0, c))],
        out_specs=pl.BlockSpec((tm, SEG), lambda c, i: (i, c)),
        out_shape=jax.ShapeDtypeStruct((n, nseg * SEG), F32),
        compiler_params=_cparams(("arbitrary", "arbitrary")),
        name="proj",
    )(x, g, w)


def _rms_rows_body(x_ref, g_ref, o_ref):
    o_ref[...] = _rms(x_ref[...], g_ref[...])


def _rms_rows(x, g):
    return pl.pallas_call(
        _rms_rows_body,
        out_shape=jax.ShapeDtypeStruct(x.shape, F32),
        name="rms_rows",
    )(x, g)


def _rwkv_pre_body(z_ref, zp_ref, f_ref, mu_ref, w0_ref, w2_ref, a0_ref, a2_ref, g2_ref,
                   kk_ref, ka_ref, rk_ref,
                   nkk_o, w_o, bb_o, k2_o, r_o, v_o, g_o, bon_o):
    j = pl.program_id(1)
    z = z_ref[...]
    bt, tt, c = z.shape
    pos = lax.broadcasted_iota(jnp.int32, z.shape, 1)
    first = jnp.where(j == 0, f_ref[...], zp_ref[:, 7:8, :])
    zprev = jnp.where(pos == 0, first, pltpu.roll(z, 1, axis=1))
    zs = (z + (zprev - z) * mu_ref[...]).reshape(bt * tt, c)
    r = zs[:, 0:512]
    k = zs[:, 512:1024]
    v = zs[:, 1024:1536]
    dl = zs[:, 1536:1664]
    al = zs[:, 1664:1792]
    gl = zs[:, 1792:1920]
    y = w0_ref[...] + _dot(jnp.tanh(dl).astype(BF16), w2_ref[...])
    softplus_neg = jnp.maximum(-y, 0.0) + jnp.log(1.0 + jnp.exp(-jnp.abs(y)))
    decay = jnp.exp(-jnp.exp(-softplus_neg - 0.5))
    a = _sigmoid(a0_ref[...] + _dot(al.astype(BF16), a2_ref[...]))
    g = _dot(_sigmoid(gl).astype(BF16), g2_ref[...])
    kk = k * kk_ref[...]
    kk = kk / jnp.maximum(jnp.sqrt(_headsum(kk * kk)), 1e-12)
    k2 = k * (1.0 + (a - 1.0) * ka_ref[...])
    bonus = _headsum(r * k2 * rk_ref[...]) * v
    shp = (bt, tt, RW_WIDTH)
    nkk_o[...] = (-kk).reshape(shp)
    w_o[...] = decay.reshape(shp)
    bb_o[...] = (kk * a).reshape(shp)
    k2_o[...] = k2.reshape(shp)
    r_o[...] = r.reshape(shp)
    v_o[...] = v.reshape(shp)
    g_o[...] = g.reshape(shp)
    bon_o[...] = bonus.reshape(shp)


def _rwkv_pre(z3, zfirst, p):
    b, t, _ = z3.shape
    if t >= 256:
        bt, tt = 1, 256
    else:
        bt, tt = min(b, 256 // t), t
    tpb = tt // 8
    row = lambda bi, j: (bi, j, 0)
    const2 = lambda bi, j: (0, 0)
    vec = pl.BlockSpec((1, RW_WIDTH), const2)
    low = pl.BlockSpec((LANES, RW_WIDTH), const2)
    out_spec = pl.BlockSpec((bt, tt, RW_WIDTH), row)
    out_shape = jax.ShapeDtypeStruct((b, t, RW_WIDTH), F32)
    return pl.pallas_call(
        _rwkv_pre_body,
        grid=(b // bt, t // tt),
        in_specs=[pl.BlockSpec((bt, tt, SEG), row),
                  pl.BlockSpec((bt, 8, SEG), lambda bi, j: (bi, jnp.maximum(j * tpb - 1, 0), 0)),
                  pl.BlockSpec((bt, 1, SEG), lambda bi, j: (bi, 0, 0)),
                  pl.BlockSpec((1, 1, SEG), lambda bi, j: (0, 0, 0)),
                  vec, low, vec, low, low, vec, vec, vec],
        out_specs=[out_spec] * 8,
        out_shape=[out_shape] * 8,
        compiler_params=_cparams(("arbitrary", "arbitrary")),
        name="rwkv_pre",
    )(z3, z3, zfirst, p["mu"], p["w0"], p["w2"], p["a0"], p["a2"], p["g2"],
      p["k_k"], p["k_a"], p["r_k"])


def _rwkv_scan_body(nkk_ref, w_ref, bb_ref, k2_ref, r_ref, vc_ref, s0_ref, o_ref, sf_ref, s_scr, *, tc):
    j = pl.program_id(1)

    @pl.when(j == 0)
    def _():
        s_scr[...] = s0_ref[0]

    def halfsum(parts):
        while len(parts) > 1:
            parts = [parts[i] + parts[i + 1] for i in range(0, len(parts), 2)]
        return parts[0] + pltpu.roll(parts[0], LANES // 2, axis=1)

    def step(t, carry):
        nkk = nkk_ref[0, t]
        wt = w_ref[0, t]
        bbt = bb_ref[0, t]
        k2t = k2_ref[0, t]
        rt = r_ref[0, t]
        vcol = vc_ref[0, t]
        sa = halfsum([s_scr[kh] * nkk[kh:kh + 1] for kh in range(SCAN_KH)])
        outs = []
        for kh in range(SCAN_KH):
            s = s_scr[kh] * wt[kh:kh + 1] + sa * bbt[kh:kh + 1] + vcol * k2t[kh:kh + 1]
            s_scr[kh] = s
            outs.append(s * rt[kh:kh + 1])
        o_ref[0, t] = halfsum(outs)
        return carry

    lax.fori_loop(0, tc, step, 0)

    @pl.when(j == pl.num_programs(1) - 1)
    def _():
        sf_ref[0] = s_scr[...]


def _rwkv_scan(rows, vcols, s0):
    g, t = vcols.shape[:2]
    tc = min(64, t)
    row = pl.BlockSpec((1, tc, SCAN_KH, LANES), lambda gi, j: (gi, j, 0, 0))
    col = pl.BlockSpec((1, tc, RW_HEAD_DIM, LANES), lambda gi, j: (gi, j, 0, 0))
    st = pl.BlockSpec((1, SCAN_KH, RW_HEAD_DIM, LANES), lambda gi, j: (gi, 0, 0, 0))
    return pl.pallas_call(
        functools.partial(_rwkv_scan_body, tc=tc),
        grid=(g, t // tc),
        in_specs=[row] * 5 + [col, st],
        out_specs=[col, st],
        out_shape=[jax.ShapeDtypeStruct(vcols.shape, F32), jax.ShapeDtypeStruct(s0.shape, F32)],
        scratch_shapes=[pltpu.VMEM((SCAN_KH, RW_HEAD_DIM, LANES), F32)],
        compiler_params=_cparams(("arbitrary", "arbitrary")),
        name="rwkv_scan",
    )(*rows, vcols, s0)


def _seq_to_lanes(x, inner):
    b, t, _ = x.shape
    x = x.reshape(b // SCAN_NB, SCAN_NB, t, inner, RW_HEADS)
    return x.transpose(0, 2, 3, 4, 1)


def _scan_layouts(nkk, w, bb, k2, r, v, wkv0):
    b, t, _ = v.shape
    g = b // SCAN_NB
    rows = [_seq_to_lanes(a, RW_HEAD_DIM).reshape(g, t, SCAN_KH, LANES) for a in (nkk, w, bb, k2, r)]
    vc = _seq_to_lanes(v, RW_HEAD_DIM).reshape(g, t, RW_HEAD_DIM, LANES // 2)
    vc = jnp.concatenate([vc, vc], axis=-1)
    if wkv0 is None:
        s0 = jnp.zeros((g, SCAN_KH, RW_HEAD_DIM, LANES), F32)
    else:
        s0 = wkv0.reshape(g, SCAN_NB, RW_HEADS, RW_HEAD_DIM, SCAN_KH, 2)
        s0 = s0.transpose(0, 4, 3, 5, 2, 1).reshape(g, SCAN_KH, RW_HEAD_DIM, LANES)
    return rows, vc, s0


def _scan_unlayout(o, sf):
    g, t = o.shape[:2]
    o = o[..., :LANES // 2].reshape(g, t, RW_HEAD_DIM, RW_HEADS, SCAN_NB)
    o = o.transpose(0, 4, 1, 2, 3).reshape(g * SCAN_NB, t, RW_WIDTH)
    sf = sf.reshape(g, SCAN_KH, RW_HEAD_DIM, 2, RW_HEADS, SCAN_NB)
    sf = sf.transpose(0, 5, 4, 2, 1, 3).reshape(g * SCAN_NB, RW_HEADS, RW_HEAD_DIM, RW_HEAD_DIM)
    return o, sf


def _hgrn_body(q_ref, f_ref, i_ref, g_ref, lbl_ref, ng_ref, st_ref, o_ref, so_ref,
               qin_s, kin_s, kout_s, d_s, *, chunk, per_seq, layer):
    rows = q_ref.shape[0]
    lg = lbl_ref[...]
    e = jnp.exp(lg - jnp.max(lg, axis=0, keepdims=True))
    sm = e / jnp.sum(e, axis=0, keepdims=True)
    lb = jnp.sum(sm[0:layer + 1], axis=0, keepdims=True)
    f = lb + (1.0 - lb) * _sigmoid(f_ref[...])
    logf = jnp.log(f)
    kf = 1.0 - f
    pos = lax.broadcasted_iota(jnp.int32, (rows, LANES), 0) % chunk
    cum = logf
    suf = logf
    s = 1
    while s < chunk:
        cum = cum + jnp.where(pos >= s, pltpu.roll(cum, s, axis=0), 0.0)
        suf = suf + jnp.where(pos < chunk - s, pltpu.roll(suf, rows - s, axis=0), 0.0)
        s *= 2
    qin_s[...] = q_ref[...] * jnp.exp(cum)
    kin_s[...] = kf * jnp.exp(-cum)
    kout_s[...] = kf * jnp.exp(suf - logf)
    d_s[...] = jnp.exp(cum + suf - logf)
    so_ref[...] = st_ref[...]

    nch = LANES // chunk
    ri = lax.broadcasted_iota(jnp.int32, (LANES, LANES), 0)
    ci = lax.broadcasted_iota(jnp.int32, (LANES, LANES), 1)
    causal = (ri // chunk == ci // chunk) & (ci <= ri)
    ng = ng_ref[...]

    def tile(ti, carry):
        r0 = pl.multiple_of(ti * LANES, LANES)
        qi = qin_s[pl.ds(r0, LANES), :]
        ki = kin_s[pl.ds(r0, LANES), :]
        ko = kout_s[pl.ds(r0, LANES), :]
        dd = d_s[pl.ds(r0, LANES), :]
        vv = i_ref[pl.ds(r0, LANES), :]
        qb = qi.astype(BF16)
        vb = vv.astype(BF16)
        att = jnp.where(causal, _dot_nt(qb, ki.astype(BF16)), 0.0)
        o = _dot(att.astype(BF16), vb)
        kot = ko.T
        if per_seq:
            sall = jnp.concatenate([so_ref[c] for c in range(nch)], axis=1).astype(BF16)
            big = _dot(qb, sall)
            inter = jnp.zeros((LANES, LANES), F32)
            for c in range(nch):
                inter = inter + jnp.where(ri // chunk == c, big[:, c * LANES:(c + 1) * LANES], 0.0)
            o = o + inter
        parts = []
        for c in range(nch):
            sidx = c if per_seq else 0
            st = so_ref[sidx]
            if not per_seq:
                parts.append(_dot(qb[c * chunk:(c + 1) * chunk], st.astype(BF16)))
            dcol = jnp.broadcast_to(dd[c * chunk:c * chunk + 1, :], (LANES, LANES)).T
            kom = jnp.where(ci // chunk == c, kot, 0.0).astype(BF16)
            so_ref[sidx] = st * dcol + _dot(kom, vb)
        if not per_seq:
            o = o + jnp.concatenate(parts, axis=0)
        o = o * lax.rsqrt(jnp.mean(o * o, axis=-1, keepdims=True) + NORM_EPS) * ng
        gate = g_ref[pl.ds(r0, LANES), :]
        o_ref[pl.ds(r0, LANES), :] = o * (gate * _sigmoid(gate))
        return carry

    lax.fori_loop(0, rows // LANES, tile, 0)


def _hgrn(z2, lb_logits, norm_g, st0, *, b, t, layer):
    chunk = min(HG_CHUNK, t)
    per_seq = t < LANES
    if per_seq:
        rows = LANES
        ns = LANES // t
    else:
        rows = t
        ns = 1
    nblk = (b * t) // rows
    c0 = SEG // LANES

    def col(off):
        return pl.BlockSpec((rows, LANES), lambda g, h: (g, c0 + off * HG_HEADS + h))

    st_spec = pl.BlockSpec((ns, None, HG_DK, HG_DV), lambda g, h: (g, h, 0, 0))
    nd = lb_logits.shape[0]
    scr = pltpu.VMEM((rows, LANES), F32)
    return pl.pallas_call(
        functools.partial(_hgrn_body, chunk=chunk, per_seq=per_seq, layer=layer),
        grid=(nblk, HG_HEADS),
        in_specs=[col(0), col(1), col(2), col(3),
                  pl.BlockSpec((nd, LANES), lambda g, h: (0, h)),
                  pl.BlockSpec((1, LANES), lambda g, h: (0, 0)),
                  st_spec],
        out_specs=[pl.BlockSpec((rows, LANES), lambda g, h: (g, h)), st_spec],
        out_shape=[jax.ShapeDtypeStruct((b * t, HG_HEADS * HG_DV), F32),
                   jax.ShapeDtypeStruct(st0.shape, F32)],
        scratch_shapes=[scr, scr, scr, scr],
        compiler_params=_cparams(("arbitrary", "arbitrary")),
        name="hgrn",
    )(z2, z2, z2, z2, lb_logits, norm_g, st0)


def _merge_body(x_ref, o_ref, bon_ref, g_ref, ob_ref, gate_ref, lnw_ref, lnb_ref,
                wa_ref, wb_ref, wo_ref, nf_ref, wq_ref, keys_ref,
                x1_o, xn_o, st_o):
    o = o_ref[...]
    d = o - _headsum(o) * (1.0 / RW_HEAD_DIM)
    var = _headsum(d * d) * (1.0 / RW_HEAD_DIM)
    oa = d * lax.rsqrt(var + RW_GN_EPS) * lnw_ref[...] + lnb_ref[...]
    oa = (oa + bon_ref[...]) * g_ref[...]
    ya = _dot(oa.astype(BF16), wa_ref[...])
    yb = _dot(ob_ref[...].astype(BF16), wb_ref[...])
    gate = gate_ref[...]
    merged = _sigmoid(gate[:, 0:D_MODEL]) * ya + _sigmoid(gate[:, D_MODEL:]) * yb
    x1 = x_ref[...] + _dot(merged.astype(BF16), wo_ref[...])
    x1_o[...] = x1
    xn = _rms(x1, nf_ref[...]).astype(BF16)
    xn_o[...] = xn
    for c in range(0, PEER_HEADS * PEER_QDIM, 512):
        q = _dot(xn, wq_ref[:, c:c + 512]).astype(BF16)
        for s in range(4):
            hc = c // LANES + s
            st_o[hc] = _dot_nt(keys_ref[hc], q[:, s * LANES:(s + 1) * LANES])


def _merge(x, o_raw, bonus, g, o_b, z2, p):
    n = x.shape[0]
    tm = min(256, n)
    nhc = 2 * PEER_HEADS
    r512 = pl.BlockSpec((tm, RW_WIDTH), lambda i: (i, 0))
    r1024 = pl.BlockSpec((tm, D_MODEL), lambda i: (i, 0))

    def full(a):
        nd = a.ndim
        return pl.BlockSpec(a.shape, lambda i: (0,) * nd)

    ws = [p["ln_w"], p["ln_b"], p["w_up_a"], p["w_up_b"], p["w_out"], p["norm_ffn_g"], p["w_q"], p["keys"]]
    return pl.pallas_call(
        _merge_body,
        grid=(n // tm,),
        in_specs=[r1024, r512, r512, r512, r512,
                  pl.BlockSpec((tm, SEG), lambda i: (i, 2))] + [full(a) for a in ws],
        out_specs=[r1024, r1024, pl.BlockSpec((nhc, PEER_KEYS, tm), lambda i: (0, 0, i))],
        out_shape=[jax.ShapeDtypeStruct((n, D_MODEL), F32),
                   jax.ShapeDtypeStruct((n, D_MODEL), BF16),
                   jax.ShapeDtypeStruct((nhc, PEER_KEYS, n), F32)],
        compiler_params=_cparams(("arbitrary",)),
        name="merge",
    )(x, o_raw, bonus, g, o_b, z2, *ws)


def _top16(xs, payloads):
    nrow = xs[0].shape[0]
    rowi = lax.broadcasted_iota(jnp.int32, xs[0].shape, 0).astype(F32)
    slot = lax.broadcasted_iota(jnp.int32, (PEER_TOPK, LANES), 0)
    zero = jnp.zeros((PEER_TOPK, LANES), F32)

    def it(r, carry):
        here = slot == r
        new = []
        for (x, vals, outs), pay in zip(carry, payloads):
            m = jnp.max(x, axis=0, keepdims=True)
            idx = jnp.min(jnp.where(x == m, rowi, float(nrow)), axis=0, keepdims=True)
            sel = rowi == idx
            vals = jnp.where(here, m, vals)
            if pay is None:
                outs = (jnp.where(here, idx, outs[0]),)
            else:
                outs = tuple(jnp.where(here, jnp.sum(jnp.where(sel, p, 0.0), axis=0, keepdims=True), o)
                             for p, o in zip(pay, outs))
            new.append((jnp.where(sel, -jnp.inf, x), vals, outs))
        return tuple(new)

    init = tuple((x, zero, tuple(zero for _ in (pay or (None,)))) for x, pay in zip(xs, payloads))
    return [(vals, outs) for _, vals, outs in lax.fori_loop(0, PEER_TOPK, it, init)]


def _pair_candidates(v1, i1, v2, i2):
    sub = 8
    row8 = lax.broadcasted_iota(jnp.int32, (sub, LANES), 0)
    cand = [v1[0:1] + v2]
    pa = [jnp.broadcast_to(i1[0:1], (PEER_TOPK, LANES))]
    pb = [i2]
    for k1 in range(1, sub):
        nvalid = PEER_TOPK // (k1 + 1)
        cand.append(jnp.where(row8 < nvalid, v1[k1:k1 + 1] + v2[0:sub], -jnp.inf))
        pa.append(jnp.broadcast_to(i1[k1:k1 + 1], (sub, LANES)))
        pb.append(i2[0:sub])
    cand.append(v1[sub:] + v2[0:1])
    pa.append(i1[sub:])
    pb.append(jnp.broadcast_to(i2[0:1], (PEER_TOPK - sub, LANES)))
    return jnp.concatenate(cand, axis=0), jnp.concatenate(pa, axis=0), jnp.concatenate(pb, axis=0)


TOPK_HEADS_PER_ROUND = 4


def _topk_body(st_ref, a_o, b_o, g_o, a_s, b_s, g_s):
    nh = TOPK_HEADS_PER_ROUND

    def group(gi, carry):
        halves = _top16([st_ref[2 * nh * gi + i] for i in range(2 * nh)], [None] * (2 * nh))
        cands = [_pair_candidates(halves[2 * i][0], halves[2 * i][1][0], halves[2 * i + 1][0], halves[2 * i + 1][1][0])
                 for i in range(nh)]
        picks = _top16([c[0] for c in cands], [(c[1], c[2]) for c in cands])
        for i, (cv, (ea, eb)) in enumerate(picks):
            ex = jnp.exp(cv - cv[0:1])
            r0 = pl.multiple_of((gi * nh + i) * PEER_TOPK, PEER_TOPK)
            a_s[pl.ds(r0, PEER_TOPK), :] = ea
            b_s[pl.ds(r0, PEER_TOPK), :] = eb
            g_s[pl.ds(r0, PEER_TOPK), :] = ex / jnp.sum(ex, axis=0, keepdims=True)
        return carry

    lax.fori_loop(0, PEER_HEADS // nh, group, 0)
    a_o[...] = a_s[...].T
    b_o[...] = b_s[...].T
    g_o[...] = g_s[...].T


def _topk(st):
    nhc, nk, n = st.shape
    out = jax.ShapeDtypeStruct((n, LANES), F32)
    ospec = pl.BlockSpec((LANES, LANES), lambda i: (i, 0))
    scr = pltpu.VMEM((LANES, LANES), F32)
    return pl.pallas_call(
        _topk_body,
        grid=(n // LANES,),
        in_specs=[pl.BlockSpec((nhc, nk, LANES), lambda i: (0, 0, i))],
        out_specs=[ospec] * 3,
        out_shape=[out] * 3,
        scratch_shapes=[scr] * 3,
        compiler_params=_cparams(("arbitrary",)),
        name="topk",
    )(st)


def _peer_body(x1_ref, xn_ref, a_ref, b_ref, gt_ref, u_ref, v_ref, nf_ref, y_ref, gate_s, acc_s, *, tb):
    c = pl.program_id(1)

    @pl.when(c == 0)
    def _():
        acc_s[...] = jnp.zeros_like(acc_s)
        sub = lax.broadcasted_iota(jnp.int32, (PEER_KEYS, LANES), 0).astype(F32)

        def tok(t, carry):
            ar = a_ref[pl.ds(t, 1), :]
            br = b_ref[pl.ds(t, 1), :]
            gr = gt_ref[pl.ds(t, 1), :]
            oa = jnp.where(sub == ar, 1.0, 0.0).astype(BF16)
            ob = jnp.where(sub == br, gr, 0.0).astype(BF16)
            gate_s[pl.ds(pl.multiple_of(t * PEER_KEYS, PEER_KEYS), PEER_KEYS), :] = _dot_nt(oa, ob)
            return carry

        lax.fori_loop(0, tb, tok, 0, unroll=32)

    xn = xn_ref[...]
    parts = []
    for k in range(PEER_STEP // PEER_CHUNK):
        u = _dot_nt(xn, u_ref[k * PEER_CHUNK:(k + 1) * PEER_CHUNK, :])
        act = 0.5 * u * (1.0 + lax.erf(u * (2.0 ** -0.5)))
        row = c * (PEER_STEP // PEER_KEYS) + 2 * k
        g0 = gate_s[pl.ds(row, tb, stride=PEER_KEYS), :]
        g1 = gate_s[pl.ds(row + 1, tb, stride=PEER_KEYS), :]
        parts.append((jnp.concatenate([g0, g1], axis=1) * act).astype(BF16))
    acc_s[...] += _dot(jnp.concatenate(parts, axis=1), v_ref[...])

    @pl.when(c == pl.num_programs(1) - 1)
    def _():
        y_ref[...] = _rms(x1_ref[...] + acc_s[...], nf_ref[...])


def _peer(x1, xn, a_idx, b_idx, gates, u_tab, v_tab, norm_g):
    n = x1.shape[0]
    tb = min(PEER_TOKENS, n)
    ne = u_tab.shape[0]
    row = lambda i, c: (i, 0)
    return pl.pallas_call(
        functools.partial(_peer_body, tb=tb),
        grid=(n // tb, ne // PEER_STEP),
        in_specs=[pl.BlockSpec((tb, D_MODEL), row),
                  pl.BlockSpec((tb, D_MODEL), row),
                  pl.BlockSpec((tb, LANES), row),
                  pl.BlockSpec((tb, LANES), row),
                  pl.BlockSpec((tb, LANES), row),
                  pl.BlockSpec((PEER_STEP, D_MODEL), lambda i, c: (c, 0)),
                  pl.BlockSpec((PEER_STEP, D_MODEL), lambda i, c: (c, 0)),
                  pl.BlockSpec((1, D_MODEL), lambda i, c: (0, 0))],
        out_specs=pl.BlockSpec((tb, D_MODEL), row),
        out_shape=jax.ShapeDtypeStruct((n, D_MODEL), F32),
        scratch_shapes=[pltpu.VMEM((tb * PEER_KEYS, LANES), F32),
                        pltpu.VMEM((tb, D_MODEL), F32)],
        compiler_params=_cparams(("arbitrary", "arbitrary")),
        name="peer",
    )(x1, xn, a_idx, b_idx, gates, u_tab, v_tab, norm_g)


def _pad_rows(a, n):
    return jnp.concatenate([a, jnp.zeros((n - a.shape[0],) + a.shape[1:], a.dtype)], axis=0)


def _layer_params(l, w_in, rw_mu, rw_w0, rw_w2, rw_a0, rw_a2, rw_g2, rw_k_k, rw_k_a, rw_r_k, rw_ln_w,
                  rw_ln_b, w_up_a, w_up_b, w_out, norm_ffn_g, peer_w_q, peer_keys):
    hm = jnp.asarray(_HM)
    r0, k0, v0 = 0, RW_WIDTH, 2 * RW_WIDTH
    d0 = 3 * RW_WIDTH
    a0 = d0 + RW_DECAY_RANK
    g0 = a0 + RW_AAA_RANK

    def rw_layout(m):
        z64 = jnp.zeros(m.shape[:-1] + (64,), m.dtype)
        zend = jnp.zeros(m.shape[:-1] + (SEG - RW_PAD,), m.dtype)
        return jnp.concatenate([m[..., r0:k0][..., hm], m[..., k0:v0][..., hm], m[..., v0:d0][..., hm],
                                m[..., d0:a0], z64, m[..., a0:g0], z64, m[..., g0:RW_COLS], zend], axis=-1)

    w = w_in[l]
    w_all = jnp.concatenate([rw_layout(w[:, :RW_COLS]), w[:, RW_COLS:]], axis=1).astype(BF16)
    row = lambda a: a.reshape(1, -1)
    return {
        "w_all": w_all,
        "mu": rw_layout(rw_mu[l]).reshape(1, 1, SEG),
        "w0": row(rw_w0[l][hm]),
        "w2": _pad_rows(rw_w2[l][:, hm], LANES).astype(BF16),
        "a0": row(rw_a0[l][hm]),
        "a2": _pad_rows(rw_a2[l][:, hm], LANES).astype(BF16),
        "g2": rw_g2[l][:, hm].astype(BF16),
        "k_k": row(rw_k_k[l][hm]),
        "k_a": row(rw_k_a[l][hm]),
        "r_k": row(rw_r_k[l].reshape(-1)[hm]),
        "ln_w": row(rw_ln_w[l][hm]),
        "ln_b": row(rw_ln_b[l][hm]),
        "w_up_a": w_up_a[l][hm, :].astype(BF16),
        "w_up_b": w_up_b[l].astype(BF16),
        "w_out": w_out[l].astype(BF16),
        "norm_ffn_g": row(norm_ffn_g[l]),
        "w_q": peer_w_q[l].astype(BF16),
        "keys": peer_keys[l].reshape(2 * PEER_HEADS, PEER_KEYS, PEER_QDIM // 2).astype(BF16),
    }


def _layer(l, x, shift0, wkv0, hg0, p, norm_mix_g, hg_lb_logits, hg_norm_g, u_tab, v_tab, out_norm_g):
    b, t, _ = x.shape
    n = b * t
    x2 = x.reshape(n, D_MODEL)
    gmix = norm_mix_g.reshape(1, D_MODEL)
    z2 = _proj(x2, gmix, p["w_all"], do_norm=True, nseg=3)
    new_shift = _rms_rows(x[:, -1], gmix)
    if shift0 is None:
        zfirst = jnp.zeros((b, 1, SEG), F32)
        h0 = jnp.zeros((b, HG_HEADS, HG_DK, HG_DV), F32)
    else:
        zfirst = _proj(shift0, gmix, p["w_all"], do_norm=False, nseg=1).reshape(b, 1, SEG)
        h0 = hg0
    nkk, w, bb, k2, r, v, g, bonus = _rwkv_pre(z2.reshape(b, t, 3 * SEG), zfirst, p)
    o_raw, new_wkv = _scan_unlayout(*_rwkv_scan(*_scan_layouts(nkk, w, bb, k2, r, v, wkv0)))
    o_b, new_hg = _hgrn(z2, hg_lb_logits, hg_norm_g.reshape(1, HG_DV), h0, b=b, t=t, layer=l)
    flat = lambda a: a.reshape(n, RW_WIDTH)
    x1, xn, st = _merge(x2, flat(o_raw), flat(bonus), flat(g), o_b, z2, p)
    a_idx, b_idx, gates = _topk(st)
    y = _peer(x1, xn, a_idx, b_idx, gates, u_tab, v_tab, out_norm_g.reshape(1, D_MODEL))
    return y.reshape(b, t, D_MODEL), new_shift, new_wkv, new_hg


def kernel(x_prompt, x_sample, state_rwkv_shift, state_rwkv_wkv, state_hgrn, norm_mix_g, w_in, rw_mu, rw_w0,
           rw_w2, rw_a0, rw_a2, rw_g2, rw_k_k, rw_k_a, rw_r_k, rw_ln_w, rw_ln_b, hg_lb_logits, hg_norm_g,
           w_up_a, w_up_b, w_out, norm_ffn_g, peer_w_q, peer_keys, peer_u, peer_v, norm_final_g):
    depth = w_in.shape[0]
    assert depth == 1, "the final rmsnorm is fused into the last layer's expert kernel"
    outs = []
    for l in range(depth):
        p = _layer_params(l, w_in, rw_mu, rw_w0, rw_w2, rw_a0, rw_a2, rw_g2, rw_k_k, rw_k_a, rw_r_k,
                          rw_ln_w, rw_ln_b, w_up_a, w_up_b, w_out, norm_ffn_g, peer_w_q, peer_keys)
        u_tab = peer_u[l].astype(BF16)
        v_tab = peer_v[l].astype(BF16)
        common = (p, norm_mix_g[l], hg_lb_logits, hg_norm_g[l], u_tab, v_tab, norm_final_g)
        yp, p_shift, p_wkv, p_hg = _layer(l, x_prompt, None, None, None, *common)
        ys, s_shift, s_wkv, s_hg = _layer(l, x_sample, state_rwkv_shift[l], state_rwkv_wkv[l],
                                          state_hgrn[l], *common)
        outs = (yp, ys, p_shift[None], p_wkv[None], p_hg[None], s_shift[None], s_wkv[None], s_hg[None])
    return outs
```

```python
import functools

import numpy as np
import jax
import jax.numpy as jnp
from jax import lax
from jax.experimental import pallas as pl
from jax.experimental.pallas import tpu as pltpu

F32 = jnp.float32
BF16 = jnp.bfloat16

D_MODEL = 1024
RW_HEADS = 8
RW_HEAD_DIM = 64
RW_WIDTH = 512
RW_DECAY_RANK = 64
RW_AAA_RANK = 64
RW_GATE_RANK = 128
RW_COLS = 1792
RW_GN_EPS = 64e-5
HG_HEADS = 4
HG_DK = 128
HG_DV = 128
HG_COLS = 2048
HG_CHUNK = 32
GATE_COLS = 2048
PEER_KEYS = 128
PEER_HEADS = 8
PEER_TOPK = 16
PEER_QDIM = 256
NORM_EPS = 1e-6

LANES = 128
SEG = 2048
RW_PAD = 1920
SCAN_NB = 8
SCAN_KH = RW_HEAD_DIM // (LANES // (RW_HEADS * SCAN_NB))
PEER_CHUNK = 256
PEER_STEP = 1024
PEER_TOKENS = 512
VMEM_LIMIT = 60000 * 1024

_HM = (np.arange(RW_WIDTH) % RW_HEADS) * RW_HEAD_DIM + np.arange(RW_WIDTH) // RW_HEADS


def _cparams(sem):
    return pltpu.CompilerParams(dimension_semantics=sem, vmem_limit_bytes=VMEM_LIMIT)


def _sigmoid(x):
    return 1.0 / (1.0 + jnp.exp(-x))


def _rms(x, g):
    return x * lax.rsqrt(jnp.mean(x * x, axis=-1, keepdims=True) + NORM_EPS) * g


def _headsum(x):
    s = x[:, 0:128] + x[:, 128:256] + x[:, 256:384] + x[:, 384:512]
    for sh in (64, 32, 16, 8):
        s = s + pltpu.roll(s, sh, axis=1)
    return jnp.concatenate([s, s, s, s], axis=1)


def _dot(a, b):
    return jnp.dot(a, b, preferred_element_type=F32)


def _dot_nt(a, b):
    return lax.dot_general(a, b, (((1,), (1,)), ((), ())), preferred_element_type=F32)


def _proj_body(x_ref, g_ref, w_ref, o_ref, *, do_norm):
    x = x_ref[...]
    if do_norm:
        x = _rms(x, g_ref[...])
    xb = x.astype(BF16)
    for c in range(0, SEG, 512):
        o_ref[:, c:c + 512] = _dot(xb, w_ref[:, c:c + 512])


def _proj(x, g, w, *, do_norm, nseg):
    n = x.shape[0]
    tm = min(512, n)
    return pl.pallas_call(
        functools.partial(_proj_body, do_norm=do_norm),
        grid=(nseg, n // tm),
        in_specs=[pl.BlockSpec((tm, D_MODEL), lambda c, i: (i, 0)),
                  pl.BlockSpec((1, D_MODEL), lambda c, i: (0, 0)),
                  pl.BlockSpec((D_MODEL, SEG), lambda c, i: (0, c))],
        out_specs=pl.BlockSpec((tm, SEG), lambda c, i: (i, c)),
        out_shape=jax.ShapeDtypeStruct((n, nseg * SEG), F32),
        compiler_params=_cparams(("arbitrary", "arbitrary")),
        name="proj",
    )(x, g, w)


def _rms_rows_body(x_ref, g_ref, o_ref):
    o_ref[...] = _rms(x_ref[...], g_ref[...])


def _rms_rows(x, g):
    return pl.pallas_call(
        _rms_rows_body,
        out_shape=jax.ShapeDtypeStruct(x.shape, F32),
        name="rms_rows",
    )(x, g)


def _rwkv_pre_body(z_ref, zp_ref, f_ref, mu_ref, w0_ref, w2_ref, a0_ref, a2_ref, g2_ref,
                   kk_ref, ka_ref, rk_ref,
                   nkk_o, w_o, bb_o, k2_o, r_o, v_o, g_o, bon_o):
    j = pl.program_id(1)
    z = z_ref[...]
    bt, tt, c = z.shape
    pos = lax.broadcasted_iota(jnp.int32, z.shape, 1)
    first = jnp.where(j == 0, f_ref[...], zp_ref[:, 7:8, :])
    zprev = jnp.where(pos == 0, first, pltpu.roll(z, 1, axis=1))
    zs = (z + (zprev - z) * mu_ref[...]).reshape(bt * tt, c)
    r = zs[:, 0:512]
    k = zs[:, 512:1024]
    v = zs[:, 1024:1536]
    dl = zs[:, 1536:1664]
    al = zs[:, 1664:1792]
    gl = zs[:, 1792:1920]
    y = w0_ref[...] + _dot(jnp.tanh(dl).astype(BF16), w2_ref[...])
    softplus_neg = jnp.maximum(-y, 0.0) + jnp.log(1.0 + jnp.exp(-jnp.abs(y)))
    decay = jnp.exp(-jnp.exp(-softplus_neg - 0.5))
    a = _sigmoid(a0_ref[...] + _dot(al.astype(BF16), a2_ref[...]))
    g = _dot(_sigmoid(gl).astype(BF16), g2_ref[...])
    kk = k * kk_ref[...]
    kk = kk / jnp.maximum(jnp.sqrt(_headsum(kk * kk)), 1e-12)
    k2 = k * (1.0 + (a - 1.0) * ka_ref[...])
    bonus = _headsum(r * k2 * rk_ref[...]) * v
    for ref, val in ((nkk_o, -kk), (w_o, decay), (bb_o, kk * a), (k2_o, k2), (r_o, r), (v_o, v), (g_o, g),
                     (bon_o, bonus)):
        ref[...] = val.reshape(ref.shape)


def _rwkv_pre(z3, zfirst, p):
    b, t, _ = z3.shape
    if t >= 256:
        bt, tt = 1, 256
    else:
        bt, tt = min(b, 256 // t), t
    tpb = tt // 8
    row = lambda bi, j: (bi, j, 0)
    const2 = lambda bi, j: (0, 0)
    vec = pl.BlockSpec((1, RW_WIDTH), const2)
    low = pl.BlockSpec((LANES, RW_WIDTH), const2)
    out_spec = pl.BlockSpec((bt, tt, RW_WIDTH), row)
    out_shape = jax.ShapeDtypeStruct((b, t, RW_WIDTH), F32)
    if bt == 1:
        scan_spec = pl.BlockSpec((None, tt, RW_WIDTH), lambda bi, j: (bi // SCAN_NB, j, bi % SCAN_NB))
        scan_shape = jax.ShapeDtypeStruct((b // SCAN_NB, t, SCAN_NB * RW_WIDTH), F32)
    else:
        scan_spec, scan_shape = out_spec, out_shape
    return pl.pallas_call(
        _rwkv_pre_body,
        grid=(b // bt, t // tt),
        in_specs=[pl.BlockSpec((bt, tt, SEG), row),
                  pl.BlockSpec((bt, 8, SEG), lambda bi, j: (bi, jnp.maximum(j * tpb - 1, 0), 0)),
                  pl.BlockSpec((bt, 1, SEG), lambda bi, j: (bi, 0, 0)),
                  pl.BlockSpec((1, 1, SEG), lambda bi, j: (0, 0, 0)),
                  vec, low, vec, low, low, vec, vec, vec],
        out_specs=[scan_spec] * 6 + [out_spec] * 2,
        out_shape=[scan_shape] * 6 + [out_shape] * 2,
        compiler_params=_cparams(("arbitrary", "arbitrary")),
        name="rwkv_pre",
    )(z3, z3, zfirst, p["mu"], p["w0"], p["w2"], p["a0"], p["a2"], p["g2"],
      p["k_k"], p["k_a"], p["r_k"])


def _rwkv_scan_body(nkk_ref, w_ref, bb_ref, k2_ref, r_ref, vc_ref, s0_ref, o_ref, sf_ref, s_scr, *, tc):
    j = pl.program_id(1)

    @pl.when(j == 0)
    def _():
        s_scr[...] = s0_ref[0]

    def halfsum(parts):
        while len(parts) > 1:
            parts = [parts[i] + parts[i + 1] for i in range(0, len(parts), 2)]
        return parts[0] + pltpu.roll(parts[0], LANES // 2, axis=1)

    def step(t, carry):
        nkk = nkk_ref[0, t]
        wt = w_ref[0, t]
        bbt = bb_ref[0, t]
        k2t = k2_ref[0, t]
        rt = r_ref[0, t]
        vcol = vc_ref[0, t]
        sa = halfsum([s_scr[kh] * nkk[kh:kh + 1] for kh in range(SCAN_KH)])
        outs = []
        for kh in range(SCAN_KH):
            s = s_scr[kh] * wt[kh:kh + 1] + sa * bbt[kh:kh + 1] + vcol * k2t[kh:kh + 1]
            s_scr[kh] = s
            outs.append(s * rt[kh:kh + 1])
        o_ref[0, t] = halfsum(outs)
        return carry

    lax.fori_loop(0, tc, step, 0)

    @pl.when(j == pl.num_programs(1) - 1)
    def _():
        sf_ref[0] = s_scr[...]


def _rwkv_scan(rows, vcols, s0):
    g, t = vcols.shape[:2]
    tc = min(64, t)
    row = pl.BlockSpec((1, tc, SCAN_KH, LANES), lambda gi, j: (gi, j, 0, 0))
    col = pl.BlockSpec((1, tc, RW_HEAD_DIM, LANES), lambda gi, j: (gi, j, 0, 0))
    st = pl.BlockSpec((1, SCAN_KH, RW_HEAD_DIM, LANES), lambda gi, j: (gi, 0, 0, 0))
    return pl.pallas_call(
        functools.partial(_rwkv_scan_body, tc=tc),
        grid=(g, t // tc),
        in_specs=[row] * 5 + [col, st],
        out_specs=[col, st],
        out_shape=[jax.ShapeDtypeStruct(vcols.shape, F32), jax.ShapeDtypeStruct(s0.shape, F32)],
        scratch_shapes=[pltpu.VMEM((SCAN_KH, RW_HEAD_DIM, LANES), F32)],
        compiler_params=_cparams(("arbitrary", "arbitrary")),
        name="rwkv_scan",
    )(*rows, vcols, s0)


def _seq_to_lanes(x):
    if x.shape[-1] == RW_WIDTH:
        b, t, c = x.shape
        x = x.reshape(b // SCAN_NB, SCAN_NB, t, c).transpose(0, 2, 1, 3)
    else:
        x = x.reshape(x.shape[0], x.shape[1], SCAN_NB, RW_WIDTH)
    return x.transpose(0, 1, 3, 2)


def _scan_layouts(nkk, w, bb, k2, r, v, wkv0):
    rows = [_seq_to_lanes(a) for a in (nkk, w, bb, k2, r)]
    g, t = rows[0].shape[:2]
    rows = [a.reshape(g, t, SCAN_KH, LANES) for a in rows]
    vc = _seq_to_lanes(v).reshape(g, t, RW_HEAD_DIM, LANES // 2)
    vc = jnp.concatenate([vc, vc], axis=-1)
    if wkv0 is None:
        s0 = jnp.zeros((g, SCAN_KH, RW_HEAD_DIM, LANES), F32)
    else:
        s0 = wkv0.reshape(g, SCAN_NB, RW_HEADS, RW_HEAD_DIM, SCAN_KH, 2)
        s0 = s0.transpose(0, 4, 3, 5, 2, 1).reshape(g, SCAN_KH, RW_HEAD_DIM, LANES)
    return rows, vc, s0


def _scan_unlayout(o, sf, time_major):
    g, t = o.shape[:2]
    o = o[..., :LANES // 2].reshape(g, t, RW_WIDTH, SCAN_NB).transpose(0, 1, 3, 2)
    if time_major:
        o = o.reshape(g, t, SCAN_NB * RW_WIDTH)
    else:
        o = o.transpose(0, 2, 1, 3).reshape(g * SCAN_NB, t, RW_WIDTH)
    sf = sf.reshape(g, SCAN_KH, RW_HEAD_DIM, 2, RW_HEADS, SCAN_NB)
    sf = sf.transpose(0, 5, 4, 2, 1, 3).reshape(g * SCAN_NB, RW_HEADS, RW_HEAD_DIM, RW_HEAD_DIM)
    return o, sf


def _hgrn_body(q_ref, f_ref, i_ref, g_ref, lbl_ref, ng_ref, st_ref, o_ref, so_ref,
               qin_s, kin_s, kout_s, d_s, *, chunk, per_seq, layer):
    rows = q_ref.shape[0]
    lg = lbl_ref[...]
    e = jnp.exp(lg - jnp.max(lg, axis=0, keepdims=True))
    sm = e / jnp.sum(e, axis=0, keepdims=True)
    lb = jnp.sum(sm[0:layer + 1], axis=0, keepdims=True)
    f = lb + (1.0 - lb) * _sigmoid(f_ref[...])
    logf = jnp.log(f)
    kf = 1.0 - f
    pos = lax.broadcasted_iota(jnp.int32, (rows, LANES), 0) % chunk
    cum = logf
    suf = logf
    s = 1
    while s < chunk:
        cum = cum + jnp.where(pos >= s, pltpu.roll(cum, s, axis=0), 0.0)
        suf = suf + jnp.where(pos < chunk - s, pltpu.roll(suf, rows - s, axis=0), 0.0)
        s *= 2
    qin_s[...] = q_ref[...] * jnp.exp(cum)
    kin_s[...] = kf * jnp.exp(-cum)
    kout_s[...] = kf * jnp.exp(suf - logf)
    d_s[...] = jnp.exp(cum + suf - logf)
    so_ref[...] = st_ref[...]

    nch = LANES // chunk
    ri = lax.broadcasted_iota(jnp.int32, (LANES, LANES), 0)
    ci = lax.broadcasted_iota(jnp.int32, (LANES, LANES), 1)
    causal = (ri // chunk == ci // chunk) & (ci <= ri)
    ng = ng_ref[...]

    def tile(ti, carry):
        r0 = pl.multiple_of(ti * LANES, LANES)
        qi = qin_s[pl.ds(r0, LANES), :]
        ki = kin_s[pl.ds(r0, LANES), :]
        ko = kout_s[pl.ds(r0, LANES), :]
        dd = d_s[pl.ds(r0, LANES), :]
        vv = i_ref[pl.ds(r0, LANES), :]
        qb = qi.astype(BF16)
        vb = vv.astype(BF16)
        att = jnp.where(causal, _dot_nt(qb, ki.astype(BF16)), 0.0)
        o = _dot(att.astype(BF16), vb)
        kot = ko.T
        if per_seq:
            sall = jnp.concatenate([so_ref[c] for c in range(nch)], axis=1).astype(BF16)
            big = _dot(qb, sall)
            inter = jnp.zeros((LANES, LANES), F32)
            for c in range(nch):
                inter = inter + jnp.where(ri // chunk == c, big[:, c * LANES:(c + 1) * LANES], 0.0)
            o = o + inter
        parts = []
        for c in range(nch):
            sidx = c if per_seq else 0
            st = so_ref[sidx]
            if not per_seq:
                parts.append(_dot(qb[c * chunk:(c + 1) * chunk], st.astype(BF16)))
            dcol = jnp.broadcast_to(dd[c * chunk:c * chunk + 1, :], (LANES, LANES)).T
            kom = jnp.where(ci // chunk == c, kot, 0.0).astype(BF16)
            so_ref[sidx] = st * dcol + _dot(kom, vb)
        if not per_seq:
            o = o + jnp.concatenate(parts, axis=0)
        o = o * lax.rsqrt(jnp.mean(o * o, axis=-1, keepdims=True) + NORM_EPS) * ng
        gate = g_ref[pl.ds(r0, LANES), :]
        o_ref[pl.ds(r0, LANES), :] = o * (gate * _sigmoid(gate))
        return carry

    lax.fori_loop(0, rows // LANES, tile, 0)


def _hgrn(z2, lb_logits, norm_g, st0, *, b, t, layer):
    chunk = min(HG_CHUNK, t)
    per_seq = t < LANES
    if per_seq:
        rows = LANES
        ns = LANES // t
    else:
        rows = t
        ns = 1
    nblk = (b * t) // rows
    c0 = SEG // LANES

    def col(off):
        return pl.BlockSpec((rows, LANES), lambda g, h: (g, c0 + off * HG_HEADS + h))

    st_spec = pl.BlockSpec((ns, None, HG_DK, HG_DV), lambda g, h: (g, h, 0, 0))
    nd = lb_logits.shape[0]
    scr = pltpu.VMEM((rows, LANES), F32)
    return pl.pallas_call(
        functools.partial(_hgrn_body, chunk=chunk, per_seq=per_seq, layer=layer),
        grid=(nblk, HG_HEADS),
        in_specs=[col(0), col(1), col(2), col(3),
                  pl.BlockSpec((nd, LANES), lambda g, h: (0, h)),
                  pl.BlockSpec((1, LANES), lambda g, h: (0, 0)),
                  st_spec],
        out_specs=[pl.BlockSpec((rows, LANES), lambda g, h: (g, h)), st_spec],
        out_shape=[jax.ShapeDtypeStruct((b * t, HG_HEADS * HG_DV), F32),
                   jax.ShapeDtypeStruct(st0.shape, F32)],
        scratch_shapes=[scr, scr, scr, scr],
        compiler_params=_cparams(("arbitrary", "arbitrary")),
        name="hgrn",
    )(z2, z2, z2, z2, lb_logits, norm_g, st0)


def _merge_body(x_ref, o_ref, bon_ref, g_ref, ob_ref, gate_ref, lnw_ref, lnb_ref,
                wa_ref, wb_ref, wo_ref, nf_ref, wq_ref, keys_ref,
                x1_o, xn_o, st_o):
    o = o_ref[...]
    d = o - _headsum(o) * (1.0 / RW_HEAD_DIM)
    var = _headsum(d * d) * (1.0 / RW_HEAD_DIM)
    oa = d * lax.rsqrt(var + RW_GN_EPS) * lnw_ref[...] + lnb_ref[...]
    oa = (oa + bon_ref[...]) * g_ref[...]
    ya = _dot(oa.astype(BF16), wa_ref[...])
    yb = _dot(ob_ref[...].astype(BF16), wb_ref[...])
    gate = gate_ref[...]
    merged = _sigmoid(gate[:, 0:D_MODEL]) * ya + _sigmoid(gate[:, D_MODEL:]) * yb
    x1 = x_ref[...] + _dot(merged.astype(BF16), wo_ref[...])
    x1_o[...] = x1
    xn = _rms(x1, nf_ref[...]).astype(BF16)
    xn_o[...] = xn
    for c in range(0, PEER_HEADS * PEER_QDIM, 512):
        q = _dot(xn, wq_ref[:, c:c + 512]).astype(BF16)
        for s in range(4):
            hc = c // LANES + s
            st_o[hc] = _dot_nt(keys_ref[hc], q[:, s * LANES:(s + 1) * LANES])


def _merge(x, o_raw, bonus, g, o_b, z2, p):
    n = x.shape[0]
    tm = min(256, n)
    nhc = 2 * PEER_HEADS
    r512 = pl.BlockSpec((tm, RW_WIDTH), lambda i: (i, 0))
    r1024 = pl.BlockSpec((tm, D_MODEL), lambda i: (i, 0))

    def full(a):
        nd = a.ndim
        return pl.BlockSpec(a.shape, lambda i: (0,) * nd)

    ws = [p["ln_w"], p["ln_b"], p["w_up_a"], p["w_up_b"], p["w_out"], p["norm_ffn_g"], p["w_q"], p["keys"]]
    if o_raw.ndim == 3:
        tiles = o_raw.shape[1] // tm
        o_spec = pl.BlockSpec((None, tm, RW_WIDTH),
                              lambda i: (i // (tiles * SCAN_NB), i % tiles, (i // tiles) % SCAN_NB))
    else:
        o_spec = r512
    return pl.pallas_call(
        _merge_body,
        grid=(n // tm,),
        in_specs=[r1024, o_spec, r512, r512, r512,
                  pl.BlockSpec((tm, SEG), lambda i: (i, 2))] + [full(a) for a in ws],
        out_specs=[r1024, r1024, pl.BlockSpec((nhc, PEER_KEYS, tm), lambda i: (0, 0, i))],
        out_shape=[jax.ShapeDtypeStruct((n, D_MODEL), F32),
                   jax.ShapeDtypeStruct((n, D_MODEL), BF16),
                   jax.ShapeDtypeStruct((nhc, PEER_KEYS, n), F32)],
        compiler_params=_cparams(("arbitrary",)),
        name="merge",
    )(x, o_raw, bonus, g, o_b, z2, *ws)


def _top16(xs, payloads):
    nrow = xs[0].shape[0]
    rowi = lax.broadcasted_iota(jnp.int32, xs[0].shape, 0).astype(F32)
    slot = lax.broadcasted_iota(jnp.int32, (PEER_TOPK, LANES), 0)
    zero = jnp.zeros((PEER_TOPK, LANES), F32)

    def it(r, carry):
        here = slot == r
        new = []
        for (x, vals, outs), pay in zip(carry, payloads):
            m = jnp.max(x, axis=0, keepdims=True)
            idx = jnp.min(jnp.where(x == m, rowi, float(nrow)), axis=0, keepdims=True)
            sel = rowi == idx
            vals = jnp.where(here, m, vals)
            if pay is None:
                outs = (jnp.where(here, idx, outs[0]),)
            else:
                outs = tuple(jnp.where(here, jnp.sum(jnp.where(sel, p, 0.0), axis=0, keepdims=True), o)
                             for p, o in zip(pay, outs))
            new.append((jnp.where(sel, -jnp.inf, x), vals, outs))
        return tuple(new)

    init = tuple((x, zero, tuple(zero for _ in (pay or (None,)))) for x, pay in zip(xs, payloads))
    return [(vals, outs) for _, vals, outs in lax.fori_loop(0, PEER_TOPK, it, init)]


def _pair_candidates(v1, i1, v2, i2):
    sub = 8
    row8 = lax.broadcasted_iota(jnp.int32, (sub, LANES), 0)
    cand = [v1[0:1] + v2]
    pa = [jnp.broadcast_to(i1[0:1], (PEER_TOPK, LANES))]
    pb = [i2]
    for k1 in range(1, sub):
        nvalid = PEER_TOPK // (k1 + 1)
        cand.append(jnp.where(row8 < nvalid, v1[k1:k1 + 1] + v2[0:sub], -jnp.inf))
        pa.append(jnp.broadcast_to(i1[k1:k1 + 1], (sub, LANES)))
        pb.append(i2[0:sub])
    cand.append(v1[sub:] + v2[0:1])
    pa.append(i1[sub:])
    pb.append(jnp.broadcast_to(i2[0:1], (PEER_TOPK - sub, LANES)))
    return jnp.concatenate(cand, axis=0), jnp.concatenate(pa, axis=0), jnp.concatenate(pb, axis=0)


TOPK_HEADS_PER_ROUND = 4


def _topk_body(st_ref, a_o, b_o, g_o, a_s, b_s, g_s):
    nh = TOPK_HEADS_PER_ROUND

    def group(gi, carry):
        halves = _top16([st_ref[2 * nh * gi + i] for i in range(2 * nh)], [None] * (2 * nh))
        cands = [_pair_candidates(halves[2 * i][0], halves[2 * i][1][0], halves[2 * i + 1][0], halves[2 * i + 1][1][0])
                 for i in range(nh)]
        picks = _top16([c[0] for c in cands], [(c[1], c[2]) for c in cands])
        for i, (cv, (ea, eb)) in enumerate(picks):
            ex = jnp.exp(cv - cv[0:1])
            r0 = pl.multiple_of((gi * nh + i) * PEER_TOPK, PEER_TOPK)
            a_s[pl.ds(r0, PEER_TOPK), :] = ea
            b_s[pl.ds(r0, PEER_TOPK), :] = eb
            g_s[pl.ds(r0, PEER_TOPK), :] = ex / jnp.sum(ex, axis=0, keepdims=True)
        return carry

    lax.fori_loop(0, PEER_HEADS // nh, group, 0)
    a_o[...] = a_s[...].T
    b_o[...] = b_s[...].T
    g_o[...] = g_s[...].T


def _topk(st):
    nhc, nk, n = st.shape
    out = jax.ShapeDtypeStruct((n, LANES), F32)
    ospec = pl.BlockSpec((LANES, LANES), lambda i: (i, 0))
    scr = pltpu.VMEM((LANES, LANES), F32)
    return pl.pallas_call(
        _topk_body,
        grid=(n // LANES,),
        in_specs=[pl.BlockSpec((nhc, nk, LANES), lambda i: (0, 0, i))],
        out_specs=[ospec] * 3,
        out_shape=[out] * 3,
        scratch_shapes=[scr] * 3,
        compiler_params=_cparams(("arbitrary",)),
        name="topk",
    )(st)


def _peer_body(x1_ref, xn_ref, a_ref, b_ref, gt_ref, u_ref, v_ref, nf_ref, y_ref, gate_s, acc_s, *, tb):
    c = pl.program_id(1)

    @pl.when(c == 0)
    def _():
        acc_s[...] = jnp.zeros_like(acc_s)
        sub = lax.broadcasted_iota(jnp.int32, (PEER_KEYS, LANES), 0).astype(F32)

        def tok(t, carry):
            ar = a_ref[pl.ds(t, 1), :]
            br = b_ref[pl.ds(t, 1), :]
            gr = gt_ref[pl.ds(t, 1), :]
            oa = jnp.where(sub == ar, 1.0, 0.0).astype(BF16)
            ob = jnp.where(sub == br, gr, 0.0).astype(BF16)
            gate_s[pl.ds(pl.multiple_of(t * PEER_KEYS, PEER_KEYS), PEER_KEYS), :] = _dot_nt(oa, ob)
            return carry

        lax.fori_loop(0, tb, tok, 0, unroll=64)

    xn = xn_ref[...]
    parts = []
    for k in range(PEER_STEP // PEER_CHUNK):
        u = _dot_nt(xn, u_ref[k * PEER_CHUNK:(k + 1) * PEER_CHUNK, :])
        act = 0.5 * u * (1.0 + lax.erf(u * (2.0 ** -0.5)))
        row = c * (PEER_STEP // PEER_KEYS) + 2 * k
        g0 = gate_s[pl.ds(row, tb, stride=PEER_KEYS), :]
        g1 = gate_s[pl.ds(row + 1, tb, stride=PEER_KEYS), :]
        parts.append((jnp.concatenate([g0, g1], axis=1) * act).astype(BF16))
    acc_s[...] += _dot(jnp.concatenate(parts, axis=1), v_ref[...])

    @pl.when(c == pl.num_programs(1) - 1)
    def _():
        y_ref[...] = _rms(x1_ref[...] + acc_s[...], nf_ref[...])


def _peer(x1, xn, a_idx, b_idx, gates, u_tab, v_tab, norm_g):
    n = x1.shape[0]
    tb = min(PEER_TOKENS, n)
    ne = u_tab.shape[0]
    row = lambda i, c: (i, 0)
    return pl.pallas_call(
        functools.partial(_peer_body, tb=tb),
        grid=(n // tb, ne // PEER_STEP),
        in_specs=[pl.BlockSpec((tb, D_MODEL), row),
                  pl.BlockSpec((tb, D_MODEL), row),
                  pl.BlockSpec((tb, LANES), row),
                  pl.BlockSpec((tb, LANES), row),
                  pl.BlockSpec((tb, LANES), row),
                  pl.BlockSpec((PEER_STEP, D_MODEL), lambda i, c: (c, 0)),
                  pl.BlockSpec((PEER_STEP, D_MODEL), lambda i, c: (c, 0)),
                  pl.BlockSpec((1, D_MODEL), lambda i, c: (0, 0))],
        out_specs=pl.BlockSpec((tb, D_MODEL), row),
        out_shape=jax.ShapeDtypeStruct((n, D_MODEL), F32),
        scratch_shapes=[pltpu.VMEM((tb * PEER_KEYS, LANES), F32),
                        pltpu.VMEM((tb, D_MODEL), F32)],
        compiler_params=_cparams(("arbitrary", "arbitrary")),
        name="peer",
    )(x1, xn, a_idx, b_idx, gates, u_tab, v_tab, norm_g)


def _pad_rows(a, n):
    return jnp.concatenate([a, jnp.zeros((n - a.shape[0],) + a.shape[1:], a.dtype)], axis=0)


def _layer_params(l, w_in, rw_mu, rw_w0, rw_w2, rw_a0, rw_a2, rw_g2, rw_k_k, rw_k_a, rw_r_k, rw_ln_w,
                  rw_ln_b, w_up_a, w_up_b, w_out, norm_ffn_g, peer_w_q, peer_keys):
    hm = jnp.asarray(_HM)
    r0, k0, v0 = 0, RW_WIDTH, 2 * RW_WIDTH
    d0 = 3 * RW_WIDTH
    a0 = d0 + RW_DECAY_RANK
    g0 = a0 + RW_AAA_RANK

    def rw_layout(m):
        z64 = jnp.zeros(m.shape[:-1] + (64,), m.dtype)
        zend = jnp.zeros(m.shape[:-1] + (SEG - RW_PAD,), m.dtype)
        return jnp.concatenate([m[..., r0:k0][..., hm], m[..., k0:v0][..., hm], m[..., v0:d0][..., hm],
                                m[..., d0:a0], z64, m[..., a0:g0], z64, m[..., g0:RW_COLS], zend], axis=-1)

    w = w_in[l]
    w_all = jnp.concatenate([rw_layout(w[:, :RW_COLS]), w[:, RW_COLS:]], axis=1).astype(BF16)
    row = lambda a: a.reshape(1, -1)
    return {
        "w_all": w_all,
        "mu": rw_layout(rw_mu[l]).reshape(1, 1, SEG),
        "w0": row(rw_w0[l][hm]),
        "w2": _pad_rows(rw_w2[l][:, hm], LANES).astype(BF16),
        "a0": row(rw_a0[l][hm]),
        "a2": _pad_rows(rw_a2[l][:, hm], LANES).astype(BF16),
        "g2": rw_g2[l][:, hm].astype(BF16),
        "k_k": row(rw_k_k[l][hm]),
        "k_a": row(rw_k_a[l][hm]),
        "r_k": row(rw_r_k[l].reshape(-1)[hm]),
        "ln_w": row(rw_ln_w[l][hm]),
        "ln_b": row(rw_ln_b[l][hm]),
        "w_up_a": w_up_a[l][hm, :].astype(BF16),
        "w_up_b": w_up_b[l].astype(BF16),
        "w_out": w_out[l].astype(BF16),
        "norm_ffn_g": row(norm_ffn_g[l]),
        "w_q": peer_w_q[l].astype(BF16),
        "keys": peer_keys[l].reshape(2 * PEER_HEADS, PEER_KEYS, PEER_QDIM // 2).astype(BF16),
    }


def _layer(l, x, shift0, wkv0, hg0, p, norm_mix_g, hg_lb_logits, hg_norm_g, u_tab, v_tab, out_norm_g):
    b, t, _ = x.shape
    n = b * t
    x2 = x.reshape(n, D_MODEL)
    gmix = norm_mix_g.reshape(1, D_MODEL)
    z2 = _proj(x2, gmix, p["w_all"], do_norm=True, nseg=3)
    new_shift = _rms_rows(x[:, -1], gmix)
    if shift0 is None:
        zfirst = jnp.zeros((b, 1, SEG), F32)
        h0 = jnp.zeros((b, HG_HEADS, HG_DK, HG_DV), F32)
    else:
        zfirst = _proj(shift0, gmix, p["w_all"], do_norm=False, nseg=1).reshape(b, 1, SEG)
        h0 = hg0
    nkk, w, bb, k2, r, v, g, bonus = _rwkv_pre(z2.reshape(b, t, 3 * SEG), zfirst, p)
    time_major = nkk.shape[-1] != RW_WIDTH
    o_raw, new_wkv = _scan_unlayout(*_rwkv_scan(*_scan_layouts(nkk, w, bb, k2, r, v, wkv0)), time_major=time_major)
    o_b, new_hg = _hgrn(z2, hg_lb_logits, hg_norm_g.reshape(1, HG_DV), h0, b=b, t=t, layer=l)
    flat = lambda a: a.reshape(n, RW_WIDTH)
    x1, xn, st = _merge(x2, o_raw if time_major else flat(o_raw), flat(bonus), flat(g), o_b, z2, p)
    a_idx, b_idx, gates = _topk(st)
    y = _peer(x1, xn, a_idx, b_idx, gates, u_tab, v_tab, out_norm_g.reshape(1, D_MODEL))
    return y.reshape(b, t, D_MODEL), new_shift, new_wkv, new_hg


def kernel(x_prompt, x_sample, state_rwkv_shift, state_rwkv_wkv, state_hgrn, norm_mix_g, w_in, rw_mu, rw_w0,
           rw_w2, rw_a0, rw_a2, rw_g2, rw_k_k, rw_k_a, rw_r_k, rw_ln_w, rw_ln_b, hg_lb_logits, hg_norm_g,
           w_up_a, w_up_b, w_out, norm_ffn_g, peer_w_q, peer_keys, peer_u, peer_v, norm_final_g):
    depth = w_in.shape[0]
    assert depth == 1, "the final rmsnorm is fused into the last layer's expert kernel"
    outs = []
    for l in range(depth):
        p = _layer_params(l, w_in, rw_mu, rw_w0, rw_w2, rw_a0, rw_a2, rw_g2, rw_k_k, rw_k_a, rw_r_k,
                          rw_ln_w, rw_ln_b, w_up_a, w_up_b, w_out, norm_ffn_g, peer_w_q, peer_keys)
        u_tab = peer_u[l].astype(BF16)
        v_tab = peer_v[l].astype(BF16)
        common = (p, norm_mix_g[l], hg_lb_logits, hg_norm_g[l], u_tab, v_tab, norm_final_g)
        yp, p_shift, p_wkv, p_hg = _layer(l, x_prompt, None, None, None, *common)
        ys, s_shift, s_wkv, s_hg = _layer(l, x_sample, state_rwkv_shift[l], state_rwkv_wkv[l],
                                          state_hgrn[l], *common)
        outs = (yp, ys, p_shift[None], p_wkv[None], p_hg[None], s_shift[None], s_wkv[None], s_hg[None])
    return outs
```

```python
import functools

import numpy as np
import jax
import jax.numpy as jnp
from jax import lax
from jax.experimental import pallas as pl
from jax.experimental.pallas import tpu as pltpu

F32 = jnp.float32
BF16 = jnp.bfloat16

D_MODEL = 1024
RW_HEADS = 8
RW_HEAD_DIM = 64
RW_WIDTH = 512
RW_DECAY_RANK = 64
RW_AAA_RANK = 64
RW_GATE_RANK = 128
RW_COLS = 1792
RW_GN_EPS = 64e-5
HG_HEADS = 4
HG_DK = 128
HG_DV = 128
HG_COLS = 2048
HG_CHUNK = 32
GATE_COLS = 2048
PEER_KEYS = 128
PEER_HEADS = 8
PEER_TOPK = 16
PEER_QDIM = 256
NORM_EPS = 1e-6

LANES = 128
SEG = 2048
RW_PAD = 1920
SCAN_NB = 8
SCAN_KH = RW_HEAD_DIM // (LANES // (RW_HEADS * SCAN_NB))
PEER_CHUNK = 256
PEER_STEP = 1024
PEER_TOKENS = 512
VMEM_LIMIT = 60000 * 1024

_HM = (np.arange(RW_WIDTH) % RW_HEADS) * RW_HEAD_DIM + np.arange(RW_WIDTH) // RW_HEADS


def _cparams(sem):
    return pltpu.CompilerParams(dimension_semantics=sem, vmem_limit_bytes=VMEM_LIMIT)


def _sigmoid(x):
    return 1.0 / (1.0 + jnp.exp(-x))


def _rms(x, g):
    return x * lax.rsqrt(jnp.mean(x * x, axis=-1, keepdims=True) + NORM_EPS) * g


def _headsum(x):
    s = x[:, 0:128] + x[:, 128:256] + x[:, 256:384] + x[:, 384:512]
    for sh in (64, 32, 16, 8):
        s = s + pltpu.roll(s, sh, axis=1)
    return jnp.concatenate([s, s, s, s], axis=1)


def _dot(a, b):
    return jnp.dot(a, b, preferred_element_type=F32)


def _dot_nt(a, b):
    return lax.dot_general(a, b, (((1,), (1,)), ((), ())), preferred_element_type=F32)


def _proj_body(x_ref, g_ref, w_ref, o_ref, *, do_norm):
    x = x_ref[...]
    if do_norm:
        x = _rms(x, g_ref[...])
    xb = x.astype(BF16)
    for c in range(0, SEG, 512):
        o_ref[:, c:c + 512] = _dot(xb, w_ref[:, c:c + 512])


def _proj(x, g, w, *, do_norm, nseg):
    n = x.shape[0]
    tm = min(512, n)
    return pl.pallas_call(
        functools.partial(_proj_body, do_norm=do_norm),
        grid=(nseg, n // tm),
        in_specs=[pl.BlockSpec((tm, D_MODEL), lambda c, i: (i, 0)),
                  pl.BlockSpec((1, D_MODEL), lambda c, i: (0, 0)),
                  pl.BlockSpec((D_MODEL, SEG), lambda c, i: (0, c))],
        out_specs=pl.BlockSpec((tm, SEG), lambda c, i: (i, c)),
        out_shape=jax.ShapeDtypeStruct((n, nseg * SEG), F32),
        compiler_params=_cparams(("arbitrary", "arbitrary")),
        name="proj",
    )(x, g, w)


def _rms_rows_body(x_ref, g_ref, o_ref):
    o_ref[...] = _rms(x_ref[...], g_ref[...])


def _rms_rows(x, g):
    return pl.pallas_call(
        _rms_rows_body,
        out_shape=jax.ShapeDtypeStruct(x.shape, F32),
        name="rms_rows",
    )(x, g)


def _rwkv_pre_body(z_ref, zp_ref, f_ref, mu_ref, w0_ref, w2_ref, a0_ref, a2_ref, g2_ref,
                   kk_ref, ka_ref, rk_ref,
                   nkk_o, w_o, bb_o, k2_o, r_o, v_o, g_o, bon_o):
    j = pl.program_id(1)
    z = z_ref[...]
    bt, tt, c = z.shape
    pos = lax.broadcasted_iota(jnp.int32, z.shape, 1)
    first = jnp.where(j == 0, f_ref[...], zp_ref[:, 7:8, :])
    zprev = jnp.where(pos == 0, first, pltpu.roll(z, 1, axis=1))
    zs = (z + (zprev - z) * mu_ref[...]).reshape(bt * tt, c)
    r = zs[:, 0:512]
    k = zs[:, 512:1024]
    v = zs[:, 1024:1536]
    dl = zs[:, 1536:1664]
    al = zs[:, 1664:1792]
    gl = zs[:, 1792:1920]
    y = w0_ref[...] + _dot(jnp.tanh(dl).astype(BF16), w2_ref[...])
    softplus_neg = jnp.maximum(-y, 0.0) + jnp.log(1.0 + jnp.exp(-jnp.abs(y)))
    decay = jnp.exp(-jnp.exp(-softplus_neg - 0.5))
    a = _sigmoid(a0_ref[...] + _dot(al.astype(BF16), a2_ref[...]))
    g = _dot(_sigmoid(gl).astype(BF16), g2_ref[...])
    kk = k * kk_ref[...]
    kk = kk / jnp.maximum(jnp.sqrt(_headsum(kk * kk)), 1e-12)
    k2 = k * (1.0 + (a - 1.0) * ka_ref[...])
    bonus = _headsum(r * k2 * rk_ref[...]) * v
    for ref, val in ((nkk_o, -kk), (w_o, decay), (bb_o, kk * a), (k2_o, k2), (r_o, r), (v_o, v), (g_o, g),
                     (bon_o, bonus)):
        ref[...] = val.reshape(ref.shape)


def _rwkv_pre(z3, zfirst, p):
    b, t, _ = z3.shape
    if t >= 256:
        bt, tt = 1, 256
    else:
        bt, tt = min(b, 256 // t), t
    tpb = tt // 8
    row = lambda bi, j: (bi, j, 0)
    const2 = lambda bi, j: (0, 0)
    vec = pl.BlockSpec((1, RW_WIDTH), const2)
    low = pl.BlockSpec((LANES, RW_WIDTH), const2)
    out_spec = pl.BlockSpec((bt, tt, RW_WIDTH), row)
    out_shape = jax.ShapeDtypeStruct((b, t, RW_WIDTH), F32)
    return pl.pallas_call(
        _rwkv_pre_body,
        grid=(b // bt, t // tt),
        in_specs=[pl.BlockSpec((bt, tt, SEG), row),
                  pl.BlockSpec((bt, 8, SEG), lambda bi, j: (bi, jnp.maximum(j * tpb - 1, 0), 0)),
                  pl.BlockSpec((bt, 1, SEG), lambda bi, j: (bi, 0, 0)),
                  pl.BlockSpec((1, 1, SEG), lambda bi, j: (0, 0, 0)),
                  vec, low, vec, low, low, vec, vec, vec],
        out_specs=[out_spec] * 8,
        out_shape=[out_shape] * 8,
        compiler_params=_cparams(("arbitrary", "arbitrary")),
        name="rwkv_pre",
    )(z3, z3, zfirst, p["mu"], p["w0"], p["w2"], p["a0"], p["a2"], p["g2"],
      p["k_k"], p["k_a"], p["r_k"])


def _rwkv_scan_body(nkk_ref, w_ref, bb_ref, k2_ref, r_ref, vc_ref, s0_ref, o_ref, sf_ref, s_scr, *, tc):
    j = pl.program_id(1)

    @pl.when(j == 0)
    def _():
        s_scr[...] = s0_ref[0]

    def halfsum(parts):
        while len(parts) > 1:
            parts = [parts[i] + parts[i + 1] for i in range(0, len(parts), 2)]
        return parts[0] + pltpu.roll(parts[0], LANES // 2, axis=1)

    def step(t, carry):
        nkk = nkk_ref[0, t]
        wt = w_ref[0, t]
        bbt = bb_ref[0, t]
        k2t = k2_ref[0, t]
        rt = r_ref[0, t]
        vcol = vc_ref[0, t]
        sa = halfsum([s_scr[kh] * nkk[kh:kh + 1] for kh in range(SCAN_KH)])
        outs = []
        for kh in range(SCAN_KH):
            s = s_scr[kh] * wt[kh:kh + 1] + sa * bbt[kh:kh + 1] + vcol * k2t[kh:kh + 1]
            s_scr[kh] = s
            outs.append(s * rt[kh:kh + 1])
        o_ref[0, t] = halfsum(outs)
        return carry

    lax.fori_loop(0, tc, step, 0)

    @pl.when(j == pl.num_programs(1) - 1)
    def _():
        sf_ref[0] = s_scr[...]


def _rwkv_scan(rows, vcols, s0):
    g, t = vcols.shape[:2]
    tc = min(64, t)
    row = pl.BlockSpec((1, tc, SCAN_KH, LANES), lambda gi, j: (gi, j, 0, 0))
    col = pl.BlockSpec((1, tc, RW_HEAD_DIM, LANES), lambda gi, j: (gi, j, 0, 0))
    st = pl.BlockSpec((1, SCAN_KH, RW_HEAD_DIM, LANES), lambda gi, j: (gi, 0, 0, 0))
    return pl.pallas_call(
        functools.partial(_rwkv_scan_body, tc=tc),
        grid=(g, t // tc),
        in_specs=[row] * 5 + [col, st],
        out_specs=[col, st],
        out_shape=[jax.ShapeDtypeStruct(vcols.shape, F32), jax.ShapeDtypeStruct(s0.shape, F32)],
        scratch_shapes=[pltpu.VMEM((SCAN_KH, RW_HEAD_DIM, LANES), F32)],
        compiler_params=_cparams(("arbitrary", "arbitrary")),
        name="rwkv_scan",
    )(*rows, vcols, s0)


PACK_STEPS = LANES // SCAN_NB


def _rows_to_step_lanes(x_ref, m_s, mt_s):
    for j in range(RW_WIDTH // LANES):
        for b in range(SCAN_NB):
            m_s[j, pl.ds(b, PACK_STEPS, stride=SCAN_NB), :] = x_ref[b, :, j * LANES:(j + 1) * LANES]
        mt_s[j * LANES:(j + 1) * LANES, :] = m_s[j].T


def _step_lanes_to_rows(mt_s, m_s, o_ref):
    for j in range(RW_WIDTH // LANES):
        m_s[j] = mt_s[j * LANES:(j + 1) * LANES, :].T
        for b in range(SCAN_NB):
            o_ref[b, :, j * LANES:(j + 1) * LANES] = m_s[j, pl.ds(b, PACK_STEPS, stride=SCAN_NB), :]


def _rot_groups(x, groups):
    groups %= PACK_STEPS
    return pltpu.roll(x, groups * SCAN_NB, axis=1) if groups else x


def _pack_one(x_ref, y_ref, m_s, mt_s, rot_s, nblk, dup):
    nrow = RW_WIDTH // nblk
    group = lax.broadcasted_iota(jnp.int32, (1, LANES), 1) // SCAN_NB
    _rows_to_step_lanes(x_ref, m_s, mt_s)
    for blk in range(nblk):
        rot_s[blk, 0:nrow] = _rot_groups(mt_s[pl.ds(blk, nrow, stride=nblk), :], blk)
    for t in range(PACK_STEPS):
        acc = jnp.zeros((nrow, LANES), F32)
        for blk in range(nblk):
            acc = jnp.where(group == (t + blk) % PACK_STEPS, rot_s[blk, 0:nrow], acc)
        y = _rot_groups(acc, -t)
        y_ref[0, t] = y + pltpu.roll(y, LANES // 2, axis=1) if dup else y


def _scan_pack_body(nkk_ref, w_ref, bb_ref, k2_ref, r_ref, v_ref,
                    nkk_o, w_o, bb_o, k2_o, r_o, vc_o, m_s, mt_s, rot_s):
    for x_ref, y_ref in ((nkk_ref, nkk_o), (w_ref, w_o), (bb_ref, bb_o), (k2_ref, k2_o), (r_ref, r_o)):
        _pack_one(x_ref, y_ref, m_s, mt_s, rot_s, 2 * RW_HEADS, False)
    _pack_one(v_ref, vc_o, m_s, mt_s, rot_s, RW_HEADS, True)


def _scan_unpack_body(o_ref, out_ref, m_s, mt_s, rot_s):
    lane = lax.broadcasted_iota(jnp.int32, (1, LANES), 1)
    group = lane // SCAN_NB
    for t in range(PACK_STEPS):
        rot_s[t] = _rot_groups(jnp.where(lane < LANES // 2, o_ref[0, t], 0.0), t)
    for h in range(RW_HEADS):
        acc = jnp.zeros((RW_HEAD_DIM, LANES), F32)
        for t in range(PACK_STEPS):
            acc = jnp.where(group == (h + t) % PACK_STEPS, rot_s[t], acc)
        mt_s[pl.ds(h, RW_HEAD_DIM, stride=RW_HEADS), :] = _rot_groups(acc, -h)
    _step_lanes_to_rows(mt_s, m_s, out_ref)


def _pack_scratch():
    return [pltpu.VMEM((RW_WIDTH // LANES, LANES, LANES), F32), pltpu.VMEM((RW_WIDTH, LANES), F32),
            pltpu.VMEM((PACK_STEPS, RW_HEAD_DIM, LANES), F32)]


def _scan_pack(arrs):
    b, t, _ = arrs[0].shape
    g = b // SCAN_NB
    rows = pl.BlockSpec((SCAN_NB, PACK_STEPS, RW_WIDTH), lambda gi, j: (gi, j, 0))
    kspec = pl.BlockSpec((1, PACK_STEPS, SCAN_KH, LANES), lambda gi, j: (gi, j, 0, 0))
    vspec = pl.BlockSpec((1, PACK_STEPS, RW_HEAD_DIM, LANES), lambda gi, j: (gi, j, 0, 0))
    return pl.pallas_call(
        _scan_pack_body,
        grid=(g, t // PACK_STEPS),
        in_specs=[rows] * 6,
        out_specs=[kspec] * 5 + [vspec],
        out_shape=[jax.ShapeDtypeStruct((g, t, SCAN_KH, LANES), F32)] * 5
        + [jax.ShapeDtypeStruct((g, t, RW_HEAD_DIM, LANES), F32)],
        scratch_shapes=_pack_scratch(),
        compiler_params=_cparams(("arbitrary", "arbitrary")),
        name="scan_pack",
    )(*arrs)


def _scan_unpack(o):
    g, t = o.shape[:2]
    return pl.pallas_call(
        _scan_unpack_body,
        grid=(g, t // PACK_STEPS),
        in_specs=[pl.BlockSpec((1, PACK_STEPS, RW_HEAD_DIM, LANES), lambda gi, j: (gi, j, 0, 0))],
        out_specs=pl.BlockSpec((SCAN_NB, PACK_STEPS, RW_WIDTH), lambda gi, j: (gi, j, 0)),
        out_shape=jax.ShapeDtypeStruct((g * SCAN_NB, t, RW_WIDTH), F32),
        scratch_shapes=_pack_scratch(),
        compiler_params=_cparams(("arbitrary", "arbitrary")),
        name="scan_unpack",
    )(o)


def _seq_to_lanes(x):
    b, t, c = x.shape
    return x.reshape(b // SCAN_NB, SCAN_NB, t, c).transpose(0, 2, 3, 1)


def _scan_layouts(nkk, w, bb, k2, r, v, wkv0):
    b, t, _ = v.shape
    g = b // SCAN_NB
    if t % PACK_STEPS == 0:
        *rows, vc = _scan_pack([nkk, w, bb, k2, r, v])
    else:
        rows = [_seq_to_lanes(a).reshape(g, t, SCAN_KH, LANES) for a in (nkk, w, bb, k2, r)]
        vc = _seq_to_lanes(v).reshape(g, t, RW_HEAD_DIM, LANES // 2)
        vc = jnp.concatenate([vc, vc], axis=-1)
    if wkv0 is None:
        s0 = jnp.zeros((g, SCAN_KH, RW_HEAD_DIM, LANES), F32)
    else:
        s0 = wkv0.reshape(g, SCAN_NB, RW_HEADS, RW_HEAD_DIM, SCAN_KH, 2)
        s0 = s0.transpose(0, 4, 3, 5, 2, 1).reshape(g, SCAN_KH, RW_HEAD_DIM, LANES)
    return rows, vc, s0


def _scan_unlayout(o, sf):
    g, t = o.shape[:2]
    if t % PACK_STEPS == 0:
        o = _scan_unpack(o)
    else:
        o = o[..., :LANES // 2].reshape(g, t, RW_WIDTH, SCAN_NB)
        o = o.transpose(0, 3, 1, 2).reshape(g * SCAN_NB, t, RW_WIDTH)
    sf = sf.reshape(g, SCAN_KH, RW_HEAD_DIM, 2, RW_HEADS, SCAN_NB)
    sf = sf.transpose(0, 5, 4, 2, 1, 3).reshape(g * SCAN_NB, RW_HEADS, RW_HEAD_DIM, RW_HEAD_DIM)
    return o, sf


def _hgrn_body(q_ref, f_ref, i_ref, g_ref, lbl_ref, ng_ref, st_ref, o_ref, so_ref,
               qin_s, kin_s, kout_s, d_s, *, chunk, per_seq, layer):
    rows = q_ref.shape[0]
    lg = lbl_ref[...]
    e = jnp.exp(lg - jnp.max(lg, axis=0, keepdims=True))
    sm = e / jnp.sum(e, axis=0, keepdims=True)
    lb = jnp.sum(sm[0:layer + 1], axis=0, keepdims=True)
    f = lb + (1.0 - lb) * _sigmoid(f_ref[...])
    logf = jnp.log(f)
    kf = 1.0 - f
    pos = lax.broadcasted_iota(jnp.int32, (rows, LANES), 0) % chunk
    cum = logf
    suf = logf
    s = 1
    while s < chunk:
        cum = cum + jnp.where(pos >= s, pltpu.roll(cum, s, axis=0), 0.0)
        suf = suf + jnp.where(pos < chunk - s, pltpu.roll(suf, rows - s, axis=0), 0.0)
        s *= 2
    qin_s[...] = q_ref[...] * jnp.exp(cum)
    kin_s[...] = kf * jnp.exp(-cum)
    kout_s[...] = kf * jnp.exp(suf - logf)
    d_s[...] = jnp.exp(cum + suf - logf)
    so_ref[...] = st_ref[...]

    nch = LANES // chunk
    ri = lax.broadcasted_iota(jnp.int32, (LANES, LANES), 0)
    ci = lax.broadcasted_iota(jnp.int32, (LANES, LANES), 1)
    causal = (ri // chunk == ci // chunk) & (ci <= ri)
    ng = ng_ref[...]

    def tile(ti, carry):
        r0 = pl.multiple_of(ti * LANES, LANES)
        qi = qin_s[pl.ds(r0, LANES), :]
        ki = kin_s[pl.ds(r0, LANES), :]
        ko = kout_s[pl.ds(r0, LANES), :]
        dd = d_s[pl.ds(r0, LANES), :]
        vv = i_ref[pl.ds(r0, LANES), :]
        qb = qi.astype(BF16)
        vb = vv.astype(BF16)
        att = jnp.where(causal, _dot_nt(qb, ki.astype(BF16)), 0.0)
        o = _dot(att.astype(BF16), vb)
        kot = ko.T
        if per_seq:
            sall = jnp.concatenate([so_ref[c] for c in range(nch)], axis=1).astype(BF16)
            big = _dot(qb, sall)
            inter = jnp.zeros((LANES, LANES), F32)
            for c in range(nch):
                inter = inter + jnp.where(ri // chunk == c, big[:, c * LANES:(c + 1) * LANES], 0.0)
            o = o + inter
        dcols = [jnp.broadcast_to(dd[c * chunk:c * chunk + 1, :], (LANES, LANES)).T for c in range(nch)]
        kvs = [_dot(jnp.where(ci // chunk == c, kot, 0.0).astype(BF16), vb) for c in range(nch)]
        if per_seq:
            for c in range(nch):
                so_ref[c] = so_ref[c] * dcols[c] + kvs[c]
        else:
            st = so_ref[0]
            parts = []
            for c in range(nch):
                parts.append(_dot(qb[c * chunk:(c + 1) * chunk], st.astype(BF16)))
                st = st * dcols[c] + kvs[c]
            so_ref[0] = st
            o = o + jnp.concatenate(parts, axis=0)
        o = o * lax.rsqrt(jnp.mean(o * o, axis=-1, keepdims=True) + NORM_EPS) * ng
        gate = g_ref[pl.ds(r0, LANES), :]
        o_ref[pl.ds(r0, LANES), :] = o * (gate * _sigmoid(gate))
        return carry

    ntile = rows // LANES
    lax.fori_loop(0, ntile, tile, 0, unroll=4 if ntile % 4 == 0 else 1)


def _hgrn(z2, lb_logits, norm_g, st0, *, b, t, layer):
    chunk = min(HG_CHUNK, t)
    per_seq = t < LANES
    if per_seq:
        rows = LANES
        ns = LANES // t
    else:
        rows = t
        ns = 1
    nblk = (b * t) // rows
    c0 = SEG // LANES

    def col(off):
        return pl.BlockSpec((rows, LANES), lambda g, h: (g, c0 + off * HG_HEADS + h))

    st_spec = pl.BlockSpec((ns, None, HG_DK, HG_DV), lambda g, h: (g, h, 0, 0))
    nd = lb_logits.shape[0]
    scr = pltpu.VMEM((rows, LANES), F32)
    return pl.pallas_call(
        functools.partial(_hgrn_body, chunk=chunk, per_seq=per_seq, layer=layer),
        grid=(nblk, HG_HEADS),
        in_specs=[col(0), col(1), col(2), col(3),
                  pl.BlockSpec((nd, LANES), lambda g, h: (0, h)),
                  pl.BlockSpec((1, LANES), lambda g, h: (0, 0)),
                  st_spec],
        out_specs=[pl.BlockSpec((rows, LANES), lambda g, h: (g, h)), st_spec],
        out_shape=[jax.ShapeDtypeStruct((b * t, HG_HEADS * HG_DV), F32),
                   jax.ShapeDtypeStruct(st0.shape, F32)],
        scratch_shapes=[scr, scr, scr, scr],
        compiler_params=_cparams(("arbitrary", "arbitrary")),
        name="hgrn",
    )(z2, z2, z2, z2, lb_logits, norm_g, st0)


def _merge_body(x_ref, o_ref, bon_ref, g_ref, ob_ref, gate_ref, lnw_ref, lnb_ref,
                wa_ref, wb_ref, wo_ref, nf_ref, wq_ref, keys_ref,
                x1_o, xn_o, st_o):
    o = o_ref[...]
    d = o - _headsum(o) * (1.0 / RW_HEAD_DIM)
    var = _headsum(d * d) * (1.0 / RW_HEAD_DIM)
    oa = d * lax.rsqrt(var + RW_GN_EPS) * lnw_ref[...] + lnb_ref[...]
    oa = (oa + bon_ref[...]) * g_ref[...]
    ya = _dot(oa.astype(BF16), wa_ref[...])
    yb = _dot(ob_ref[...].astype(BF16), wb_ref[...])
    gate = gate_ref[...]
    merged = _sigmoid(gate[:, 0:D_MODEL]) * ya + _sigmoid(gate[:, D_MODEL:]) * yb
    x1 = x_ref[...] + _dot(merged.astype(BF16), wo_ref[...])
    x1_o[...] = x1
    xn = _rms(x1, nf_ref[...]).astype(BF16)
    xn_o[...] = xn
    for c in range(0, PEER_HEADS * PEER_QDIM, 512):
        q = _dot(xn, wq_ref[:, c:c + 512]).astype(BF16)
        for s in range(4):
            hc = c // LANES + s
            st_o[hc] = _dot_nt(keys_ref[hc], q[:, s * LANES:(s + 1) * LANES])


def _merge(x, o_raw, bonus, g, o_b, z2, p):
    n = x.shape[0]
    tm = min(256, n)
    nhc = 2 * PEER_HEADS
    r512 = pl.BlockSpec((tm, RW_WIDTH), lambda i: (i, 0))
    r1024 = pl.BlockSpec((tm, D_MODEL), lambda i: (i, 0))

    def full(a):
        nd = a.ndim
        return pl.BlockSpec(a.shape, lambda i: (0,) * nd)

    ws = [p["ln_w"], p["ln_b"], p["w_up_a"], p["w_up_b"], p["w_out"], p["norm_ffn_g"], p["w_q"], p["keys"]]
    return pl.pallas_call(
        _merge_body,
        grid=(n // tm,),
        in_specs=[r1024, r512, r512, r512, r512,
                  pl.BlockSpec((tm, SEG), lambda i: (i, 2))] + [full(a) for a in ws],
        out_specs=[r1024, r1024, pl.BlockSpec((nhc, PEER_KEYS, tm), lambda i: (0, 0, i))],
        out_shape=[jax.ShapeDtypeStruct((n, D_MODEL), F32),
                   jax.ShapeDtypeStruct((n, D_MODEL), BF16),
                   jax.ShapeDtypeStruct((nhc, PEER_KEYS, n), F32)],
        compiler_params=_cparams(("arbitrary",)),
        name="merge",
    )(x, o_raw, bonus, g, o_b, z2, *ws)


def _top16(xs, payloads):
    nrow = xs[0].shape[0]
    rowi = lax.broadcasted_iota(jnp.int32, xs[0].shape, 0).astype(F32)
    slot = lax.broadcasted_iota(jnp.int32, (PEER_TOPK, LANES), 0)
    zero = jnp.zeros((PEER_TOPK, LANES), F32)

    def it(r, carry):
        here = slot == r
        new = []
        for (x, vals, outs), pay in zip(carry, payloads):
            m = jnp.max(x, axis=0, keepdims=True)
            idx = jnp.min(jnp.where(x == m, rowi, float(nrow)), axis=0, keepdims=True)
            sel = rowi == idx
            vals = jnp.where(here, m, vals)
            if pay is None:
                outs = (jnp.where(here, idx, outs[0]),)
            else:
                outs = tuple(jnp.where(here, jnp.sum(jnp.where(sel, p, 0.0), axis=0, keepdims=True), o)
                             for p, o in zip(pay, outs))
            new.append((jnp.where(sel, -jnp.inf, x), vals, outs))
        return tuple(new)

    init = tuple((x, zero, tuple(zero for _ in (pay or (None,)))) for x, pay in zip(xs, payloads))
    return [(vals, outs) for _, vals, outs in lax.fori_loop(0, PEER_TOPK, it, init)]


def _pair_candidates(v1, i1, v2, i2):
    sub = 8
    row8 = lax.broadcasted_iota(jnp.int32, (sub, LANES), 0)
    cand = [v1[0:1] + v2]
    pa = [jnp.broadcast_to(i1[0:1], (PEER_TOPK, LANES))]
    pb = [i2]
    for k1 in range(1, sub):
        nvalid = PEER_TOPK // (k1 + 1)
        cand.append(jnp.where(row8 < nvalid, v1[k1:k1 + 1] + v2[0:sub], -jnp.inf))
        pa.append(jnp.broadcast_to(i1[k1:k1 + 1], (sub, LANES)))
        pb.append(i2[0:sub])
    cand.append(v1[sub:] + v2[0:1])
    pa.append(i1[sub:])
    pb.append(jnp.broadcast_to(i2[0:1], (PEER_TOPK - sub, LANES)))
    return jnp.concatenate(cand, axis=0), jnp.concatenate(pa, axis=0), jnp.concatenate(pb, axis=0)


TOPK_HEADS_PER_ROUND = 4


def _topk_body(st_ref, a_o, b_o, g_o, a_s, b_s, g_s):
    nh = TOPK_HEADS_PER_ROUND

    def group(gi, carry):
        halves = _top16([st_ref[2 * nh * gi + i] for i in range(2 * nh)], [None] * (2 * nh))
        cands = [_pair_candidates(halves[2 * i][0], halves[2 * i][1][0], halves[2 * i + 1][0], halves[2 * i + 1][1][0])
                 for i in range(nh)]
        picks = _top16([c[0] for c in cands], [(c[1], c[2]) for c in cands])
        for i, (cv, (ea, eb)) in enumerate(picks):
            ex = jnp.exp(cv - cv[0:1])
            r0 = pl.multiple_of((gi * nh + i) * PEER_TOPK, PEER_TOPK)
            a_s[pl.ds(r0, PEER_TOPK), :] = ea
            b_s[pl.ds(r0, PEER_TOPK), :] = eb
            g_s[pl.ds(r0, PEER_TOPK), :] = ex / jnp.sum(ex, axis=0, keepdims=True)
        return carry

    lax.fori_loop(0, PEER_HEADS // nh, group, 0)
    a_o[...] = a_s[...].T
    b_o[...] = b_s[...].T
    g_o[...] = g_s[...].T


def _topk(st):
    nhc, nk, n = st.shape
    out = jax.ShapeDtypeStruct((n, LANES), F32)
    ospec = pl.BlockSpec((LANES, LANES), lambda i: (i, 0))
    scr = pltpu.VMEM((LANES, LANES), F32)
    return pl.pallas_call(
        _topk_body,
        grid=(n // LANES,),
        in_specs=[pl.BlockSpec((nhc, nk, LANES), lambda i: (0, 0, i))],
        out_specs=[ospec] * 3,
        out_shape=[out] * 3,
        scratch_shapes=[scr] * 3,
        compiler_params=_cparams(("arbitrary",)),
        name="topk",
    )(st)


def _peer_body(x1_ref, xn_ref, a_ref, b_ref, gt_ref, u_ref, v_ref, nf_ref, y_ref, gate_s, acc_s, *, tb):
    c = pl.program_id(1)

    @pl.when(c == 0)
    def _():
        acc_s[...] = jnp.zeros_like(acc_s)
        sub = lax.broadcasted_iota(jnp.int32, (PEER_KEYS, LANES), 0).astype(F32)

        def tok(t, carry):
            ar = a_ref[pl.ds(t, 1), :]
            br = b_ref[pl.ds(t, 1), :]
            gr = gt_ref[pl.ds(t, 1), :]
            oa = jnp.where(sub == ar, 1.0, 0.0).astype(BF16)
            ob = jnp.where(sub == br, gr, 0.0).astype(BF16)
            gate_s[pl.ds(pl.multiple_of(t * PEER_KEYS, PEER_KEYS), PEER_KEYS), :] = _dot_nt(oa, ob)
            return carry

        lax.fori_loop(0, tb, tok, 0, unroll=64)

    xn = xn_ref[...]
    parts = []
    for k in range(PEER_STEP // PEER_CHUNK):
        u = _dot_nt(xn, u_ref[k * PEER_CHUNK:(k + 1) * PEER_CHUNK, :])
        act = 0.5 * u * (1.0 + lax.erf(u * (2.0 ** -0.5)))
        row = c * (PEER_STEP // PEER_KEYS) + 2 * k
        g0 = gate_s[pl.ds(row, tb, stride=PEER_KEYS), :]
        g1 = gate_s[pl.ds(row + 1, tb, stride=PEER_KEYS), :]
        parts.append((jnp.concatenate([g0, g1], axis=1) * act).astype(BF16))
    acc_s[...] += _dot(jnp.concatenate(parts, axis=1), v_ref[...])

    @pl.when(c == pl.num_programs(1) - 1)
    def _():
        y_ref[...] = _rms(x1_ref[...] + acc_s[...], nf_ref[...])


def _peer(x1, xn, a_idx, b_idx, gates, u_tab, v_tab, norm_g):
    n = x1.shape[0]
    tb = min(PEER_TOKENS, n)
    ne = u_tab.shape[0]
    row = lambda i, c: (i, 0)
    return pl.pallas_call(
        functools.partial(_peer_body, tb=tb),
        grid=(n // tb, ne // PEER_STEP),
        in_specs=[pl.BlockSpec((tb, D_MODEL), row),
                  pl.BlockSpec((tb, D_MODEL), row),
                  pl.BlockSpec((tb, LANES), row),
                  pl.BlockSpec((tb, LANES), row),
                  pl.BlockSpec((tb, LANES), row),
                  pl.BlockSpec((PEER_STEP, D_MODEL), lambda i, c: (c, 0)),
                  pl.BlockSpec((PEER_STEP, D_MODEL), lambda i, c: (c, 0)),
                  pl.BlockSpec((1, D_MODEL), lambda i, c: (0, 0))],
        out_specs=pl.BlockSpec((tb, D_MODEL), row),
        out_shape=jax.ShapeDtypeStruct((n, D_MODEL), F32),
        scratch_shapes=[pltpu.VMEM((tb * PEER_KEYS, LANES), F32),
                        pltpu.VMEM((tb, D_MODEL), F32)],
        compiler_params=_cparams(("arbitrary", "arbitrary")),
        name="peer",
    )(x1, xn, a_idx, b_idx, gates, u_tab, v_tab, norm_g)


def _pad_rows(a, n):
    return jnp.concatenate([a, jnp.zeros((n - a.shape[0],) + a.shape[1:], a.dtype)], axis=0)


def _layer_params(l, w_in, rw_mu, rw_w0, rw_w2, rw_a0, rw_a2, rw_g2, rw_k_k, rw_k_a, rw_r_k, rw_ln_w,
                  rw_ln_b, w_up_a, w_up_b, w_out, norm_ffn_g, peer_w_q, peer_keys):
    hm = jnp.asarray(_HM)
    r0, k0, v0 = 0, RW_WIDTH, 2 * RW_WIDTH
    d0 = 3 * RW_WIDTH
    a0 = d0 + RW_DECAY_RANK
    g0 = a0 + RW_AAA_RANK

    def rw_layout(m):
        z64 = jnp.zeros(m.shape[:-1] + (64,), m.dtype)
        zend = jnp.zeros(m.shape[:-1] + (SEG - RW_PAD,), m.dtype)
        return jnp.concatenate([m[..., r0:k0][..., hm], m[..., k0:v0][..., hm], m[..., v0:d0][..., hm],
                                m[..., d0:a0], z64, m[..., a0:g0], z64, m[..., g0:RW_COLS], zend], axis=-1)

    w = w_in[l]
    w_all = jnp.concatenate([rw_layout(w[:, :RW_COLS]), w[:, RW_COLS:]], axis=1).astype(BF16)
    row = lambda a: a.reshape(1, -1)
    return {
        "w_all": w_all,
        "mu": rw_layout(rw_mu[l]).reshape(1, 1, SEG),
        "w0": row(rw_w0[l][hm]),
        "w2": _pad_rows(rw_w2[l][:, hm], LANES).astype(BF16),
        "a0": row(rw_a0[l][hm]),
        "a2": _pad_rows(rw_a2[l][:, hm], LANES).astype(BF16),
        "g2": rw_g2[l][:, hm].astype(BF16),
        "k_k": row(rw_k_k[l][hm]),
        "k_a": row(rw_k_a[l][hm]),
        "r_k": row(rw_r_k[l].reshape(-1)[hm]),
        "ln_w": row(rw_ln_w[l][hm]),
        "ln_b": row(rw_ln_b[l][hm]),
        "w_up_a": w_up_a[l][hm, :].astype(BF16),
        "w_up_b": w_up_b[l].astype(BF16),
        "w_out": w_out[l].astype(BF16),
        "norm_ffn_g": row(norm_ffn_g[l]),
        "w_q": peer_w_q[l].astype(BF16),
        "keys": peer_keys[l].reshape(2 * PEER_HEADS, PEER_KEYS, PEER_QDIM // 2).astype(BF16),
    }


def _layer(l, x, shift0, wkv0, hg0, p, norm_mix_g, hg_lb_logits, hg_norm_g, u_tab, v_tab, out_norm_g):
    b, t, _ = x.shape
    n = b * t
    x2 = x.reshape(n, D_MODEL)
    gmix = norm_mix_g.reshape(1, D_MODEL)
    z2 = _proj(x2, gmix, p["w_all"], do_norm=True, nseg=3)
    new_shift = _rms_rows(x[:, -1], gmix)
    if shift0 is None:
        zfirst = jnp.zeros((b, 1, SEG), F32)
        h0 = jnp.zeros((b, HG_HEADS, HG_DK, HG_DV), F32)
    else:
        zfirst = _proj(shift0, gmix, p["w_all"], do_norm=False, nseg=1).reshape(b, 1, SEG)
        h0 = hg0
    nkk, w, bb, k2, r, v, g, bonus = _rwkv_pre(z2.reshape(b, t, 3 * SEG), zfirst, p)
    o_raw, new_wkv = _scan_unlayout(*_rwkv_scan(*_scan_layouts(nkk, w, bb, k2, r, v, wkv0)))
    o_b, new_hg = _hgrn(z2, hg_lb_logits, hg_norm_g.reshape(1, HG_DV), h0, b=b, t=t, layer=l)
    flat = lambda a: a.reshape(n, RW_WIDTH)
    x1, xn, st = _merge(x2, flat(o_raw), flat(bonus), flat(g), o_b, z2, p)
    a_idx, b_idx, gates = _topk(st)
    y = _peer(x1, xn, a_idx, b_idx, gates, u_tab, v_tab, out_norm_g.reshape(1, D_MODEL))
    return y.reshape(b, t, D_MODEL), new_shift, new_wkv, new_hg


def kernel(x_prompt, x_sample, state_rwkv_shift, state_rwkv_wkv, state_hgrn, norm_mix_g, w_in, rw_mu, rw_w0,
           rw_w2, rw_a0, rw_a2, rw_g2, rw_k_k, rw_k_a, rw_r_k, rw_ln_w, rw_ln_b, hg_lb_logits, hg_norm_g,
           w_up_a, w_up_b, w_out, norm_ffn_g, peer_w_q, peer_keys, peer_u, peer_v, norm_final_g):
    depth = w_in.shape[0]
    assert depth == 1, "the final rmsnorm is fused into the last layer's expert kernel"
    outs = []
    for l in range(depth):
        p = _layer_params(l, w_in, rw_mu, rw_w0, rw_w2, rw_a0, rw_a2, rw_g2, rw_k_k, rw_k_a, rw_r_k,
                          rw_ln_w, rw_ln_b, w_up_a, w_up_b, w_out, norm_ffn_g, peer_w_q, peer_keys)
        u_tab = peer_u[l].astype(BF16)
        v_tab = peer_v[l].astype(BF16)
        common = (p, norm_mix_g[l], hg_lb_logits, hg_norm_g[l], u_tab, v_tab, norm_final_g)
        yp, p_shift, p_wkv, p_hg = _layer(l, x_prompt, None, None, None, *common)
        ys, s_shift, s_wkv, s_hg = _layer(l, x_sample, state_rwkv_shift[l], state_rwkv_wkv[l],
                                          state_hgrn[l], *common)
        outs = (yp, ys, p_shift[None], p_wkv[None], p_hg[None], s_shift[None], s_wkv[None], s_hg[None])
    return outs
```

```python
import functools

import numpy as np
import jax
import jax.numpy as jnp
from jax import lax
from jax.experimental import pallas as pl
from jax.experimental.pallas import tpu as pltpu

F32 = jnp.float32
BF16 = jnp.bfloat16

D_MODEL = 1024
RW_HEADS = 8
RW_HEAD_DIM = 64
RW_WIDTH = 512
RW_DECAY_RANK = 64
RW_AAA_RANK = 64
RW_GATE_RANK = 128
RW_COLS = 1792
RW_GN_EPS = 64e-5
HG_HEADS = 4
HG_DK = 128
HG_DV = 128
HG_COLS = 2048
HG_CHUNK = 32
GATE_COLS = 2048
PEER_KEYS = 128
PEER_HEADS = 8
PEER_TOPK = 16
PEER_QDIM = 256
NORM_EPS = 1e-6

LANES = 128
SEG = 2048
RW_PAD = 1920
SCAN_NB = 8
SCAN_KH = RW_HEAD_DIM // (LANES // (RW_HEADS * SCAN_NB))
SCAN_ACCS = 4
PEER_CHUNK = 256
PEER_STEP = 1024
PEER_TOKENS = 512
VMEM_LIMIT = 60000 * 1024

_HM = (np.arange(RW_WIDTH) % RW_HEADS) * RW_HEAD_DIM + np.arange(RW_WIDTH) // RW_HEADS


def _cparams(sem):
    return pltpu.CompilerParams(dimension_semantics=sem, vmem_limit_bytes=VMEM_LIMIT)


def _sigmoid(x):
    return 1.0 / (1.0 + jnp.exp(-x))


def _rms(x, g):
    return x * lax.rsqrt(jnp.mean(x * x, axis=-1, keepdims=True) + NORM_EPS) * g


def _headsum(x):
    s = x[:, 0:128] + x[:, 128:256] + x[:, 256:384] + x[:, 384:512]
    for sh in (64, 32, 16, 8):
        s = s + pltpu.roll(s, sh, axis=1)
    return jnp.concatenate([s, s, s, s], axis=1)


def _dot(a, b):
    return jnp.dot(a, b, preferred_element_type=F32)


def _dot_nt(a, b):
    return lax.dot_general(a, b, (((1,), (1,)), ((), ())), preferred_element_type=F32)


def _proj_body(x_ref, g_ref, w_ref, o_ref, *, do_norm):
    x = x_ref[...]
    if do_norm:
        x = _rms(x, g_ref[...])
    xb = x.astype(BF16)
    for c in range(0, SEG, 512):
        o_ref[:, c:c + 512] = _dot(xb, w_ref[:, c:c + 512])


def _proj(x, g, w, *, do_norm, nseg):
    n = x.shape[0]
    tm = min(512, n)
    return pl.pallas_call(
        functools.partial(_proj_body, do_norm=do_norm),
        grid=(nseg, n // tm),
        in_specs=[pl.BlockSpec((tm, D_MODEL), lambda c, i: (i, 0)),
                  pl.BlockSpec((1, D_MODEL), lambda c, i: (0, 0)),
                  pl.BlockSpec((D_MODEL, SEG), lambda c, i: (0, c))],
        out_specs=pl.BlockSpec((tm, SEG), lambda c, i: (i, c)),
        out_shape=jax.ShapeDtypeStruct((n, nseg * SEG), F32),
        compiler_params=_cparams(("arbitrary", "arbitrary")),
        name="proj",
    )(x, g, w)


def _rms_rows_body(x_ref, g_ref, o_ref):
    o_ref[...] = _rms(x_ref[...], g_ref[...])


def _rms_rows(x, g):
    return pl.pallas_call(
        _rms_rows_body,
        out_shape=jax.ShapeDtypeStruct(x.shape, F32),
        name="rms_rows",
    )(x, g)


def _rwkv_pre_body(z_ref, zp_ref, f_ref, mu_ref, w0_ref, w2_ref, a0_ref, a2_ref, g2_ref,
                   kk_ref, ka_ref, rk_ref,
                   nkk_o, w_o, bb_o, k2_o, r_o, v_o, g_o, bon_o):
    j = pl.program_id(1)
    z = z_ref[...]
    bt, tt, c = z.shape
    pos = lax.broadcasted_iota(jnp.int32, z.shape, 1)
    first = jnp.where(j == 0, f_ref[...], zp_ref[:, 7:8, :])
    zprev = jnp.where(pos == 0, first, pltpu.roll(z, 1, axis=1))
    zs = (z + (zprev - z) * mu_ref[...]).reshape(bt * tt, c)
    r = zs[:, 0:512]
    k = zs[:, 512:1024]
    v = zs[:, 1024:1536]
    dl = zs[:, 1536:1664]
    al = zs[:, 1664:1792]
    gl = zs[:, 1792:1920]
    y = w0_ref[...] + _dot(jnp.tanh(dl).astype(BF16), w2_ref[...])
    softplus_neg = jnp.maximum(-y, 0.0) + jnp.log(1.0 + jnp.exp(-jnp.abs(y)))
    decay = jnp.exp(-jnp.exp(-softplus_neg - 0.5))
    a = _sigmoid(a0_ref[...] + _dot(al.astype(BF16), a2_ref[...]))
    g = _dot(_sigmoid(gl).astype(BF16), g2_ref[...])
    kk = k * kk_ref[...]
    kk = kk / jnp.maximum(jnp.sqrt(_headsum(kk * kk)), 1e-12)
    k2 = k * (1.0 + (a - 1.0) * ka_ref[...])
    bonus = _headsum(r * k2 * rk_ref[...]) * v
    for ref, val in ((nkk_o, -kk), (w_o, decay), (bb_o, kk * a), (k2_o, k2), (r_o, r), (v_o, v), (g_o, g),
                     (bon_o, bonus)):
        ref[...] = val.reshape(ref.shape)


def _rwkv_pre(z3, zfirst, p):
    b, t, _ = z3.shape
    if t >= 256:
        bt, tt = 1, 256
    else:
        bt, tt = min(b, 256 // t), t
    tpb = tt // 8
    row = lambda bi, j: (bi, j, 0)
    const2 = lambda bi, j: (0, 0)
    vec = pl.BlockSpec((1, RW_WIDTH), const2)
    low = pl.BlockSpec((LANES, RW_WIDTH), const2)
    out_spec = pl.BlockSpec((bt, tt, RW_WIDTH), row)
    out_shape = jax.ShapeDtypeStruct((b, t, RW_WIDTH), F32)
    return pl.pallas_call(
        _rwkv_pre_body,
        grid=(b // bt, t // tt),
        in_specs=[pl.BlockSpec((bt, tt, SEG), row),
                  pl.BlockSpec((bt, 8, SEG), lambda bi, j: (bi, jnp.maximum(j * tpb - 1, 0), 0)),
                  pl.BlockSpec((bt, 1, SEG), lambda bi, j: (bi, 0, 0)),
                  pl.BlockSpec((1, 1, SEG), lambda bi, j: (0, 0, 0)),
                  vec, low, vec, low, low, vec, vec, vec],
        out_specs=[out_spec] * 8,
        out_shape=[out_shape] * 8,
        compiler_params=_cparams(("arbitrary", "arbitrary")),
        name="rwkv_pre",
    )(z3, z3, zfirst, p["mu"], p["w0"], p["w2"], p["a0"], p["a2"], p["g2"],
      p["k_k"], p["k_a"], p["r_k"])


def _rwkv_scan_body(nkk_ref, w_ref, bb_ref, k2_ref, r_ref, vc_ref, s0_ref, o_ref, sf_ref, s_scr, *, tc):
    j = pl.program_id(1)

    @pl.when(j == 0)
    def _():
        s_scr[...] = s0_ref[0]

    def halfsum(parts):
        while len(parts) > 1:
            parts = [parts[i] + parts[i + 1] for i in range(0, len(parts), 2)]
        return parts[0] + pltpu.roll(parts[0], LANES // 2, axis=1)

    def step(t, carry):
        nkk = nkk_ref[0, t]
        wt = w_ref[0, t]
        bbt = bb_ref[0, t]
        k2t = k2_ref[0, t]
        rt = r_ref[0, t]
        vcol = vc_ref[0, t]
        accs = [None] * SCAN_ACCS
        for kh in range(SCAN_KH):
            p = s_scr[kh] * nkk[kh:kh + 1]
            accs[kh % SCAN_ACCS] = p if kh < SCAN_ACCS else accs[kh % SCAN_ACCS] + p
        sa = halfsum(accs)
        accs = [None] * SCAN_ACCS
        for kh in range(SCAN_KH):
            s = s_scr[kh] * wt[kh:kh + 1] + sa * bbt[kh:kh + 1] + vcol * k2t[kh:kh + 1]
            s_scr[kh] = s
            q = s * rt[kh:kh + 1]
            accs[kh % SCAN_ACCS] = q if kh < SCAN_ACCS else accs[kh % SCAN_ACCS] + q
        o_ref[0, t] = halfsum(accs)
        return carry

    lax.fori_loop(0, tc, step, 0)

    @pl.when(j == pl.num_programs(1) - 1)
    def _():
        sf_ref[0] = s_scr[...]


def _rwkv_scan(rows, vcols, s0):
    g, t = vcols.shape[:2]
    tc = min(64, t)
    row = pl.BlockSpec((1, tc, SCAN_KH, LANES), lambda gi, j: (gi, j, 0, 0))
    col = pl.BlockSpec((1, tc, RW_HEAD_DIM, LANES), lambda gi, j: (gi, j, 0, 0))
    st = pl.BlockSpec((1, SCAN_KH, RW_HEAD_DIM, LANES), lambda gi, j: (gi, 0, 0, 0))
    return pl.pallas_call(
        functools.partial(_rwkv_scan_body, tc=tc),
        grid=(g, t // tc),
        in_specs=[row] * 5 + [col, st],
        out_specs=[col, st],
        out_shape=[jax.ShapeDtypeStruct(vcols.shape, F32), jax.ShapeDtypeStruct(s0.shape, F32)],
        scratch_shapes=[pltpu.VMEM((SCAN_KH, RW_HEAD_DIM, LANES), F32)],
        compiler_params=_cparams(("arbitrary", "arbitrary")),
        name="rwkv_scan",
    )(*rows, vcols, s0)


PACK_STEPS = LANES // SCAN_NB


def _rows_to_step_lanes(x_ref, m_s, mt_s):
    for j in range(RW_WIDTH // LANES):
        for b in range(SCAN_NB):
            m_s[j, pl.ds(b, PACK_STEPS, stride=SCAN_NB), :] = x_ref[b, :, j * LANES:(j + 1) * LANES]
        mt_s[j * LANES:(j + 1) * LANES, :] = m_s[j].T


def _step_lanes_to_rows(mt_s, m_s, o_ref):
    for j in range(RW_WIDTH // LANES):
        m_s[j] = mt_s[j * LANES:(j + 1) * LANES, :].T
        for b in range(SCAN_NB):
            o_ref[b, :, j * LANES:(j + 1) * LANES] = m_s[j, pl.ds(b, PACK_STEPS, stride=SCAN_NB), :]


def _rot_groups(x, groups):
    groups %= PACK_STEPS
    return pltpu.roll(x, groups * SCAN_NB, axis=1) if groups else x


def _gather_rotated(tiles):
    n = PACK_STEPS
    group = lax.broadcasted_iota(jnp.int32, (1, LANES), 1) // SCAN_NB
    lst = [tiles[(-m) % n] for m in range(n)]
    for k in range(n.bit_length() - 1):
        bit = ((group >> k) & 1) == 1
        lst = [jnp.where(bit, lst[(s - (1 << k)) % n], lst[s]) for s in range(n)]
    return lst


def _pack_one(x_ref, y_ref, m_s, mt_s, nblk, dup):
    nrow = RW_WIDTH // nblk
    _rows_to_step_lanes(x_ref, m_s, mt_s)
    zero = jnp.zeros((nrow, LANES), F32)
    tiles = [_rot_groups(mt_s[pl.ds(blk, nrow, stride=nblk), :], blk) if blk < nblk else zero
             for blk in range(PACK_STEPS)]
    for t, acc in enumerate(_gather_rotated(tiles)):
        y = _rot_groups(acc, -t)
        y_ref[0, t] = y + pltpu.roll(y, LANES // 2, axis=1) if dup else y


def _scan_pack_body(nkk_ref, w_ref, bb_ref, k2_ref, r_ref, v_ref,
                    nkk_o, w_o, bb_o, k2_o, r_o, vc_o, m_s, mt_s):
    pairs = ((nkk_ref, nkk_o), (w_ref, w_o), (bb_ref, bb_o), (k2_ref, k2_o), (r_ref, r_o))
    for i, (x_ref, y_ref) in enumerate(pairs):
        _pack_one(x_ref, y_ref, m_s.at[i], mt_s.at[i], 2 * RW_HEADS, False)
    _pack_one(v_ref, vc_o, m_s.at[5], mt_s.at[5], RW_HEADS, True)


def _scan_unpack_body(o_ref, out_ref, m_s, mt_s):
    m_s, mt_s = m_s.at[0], mt_s.at[0]
    lane = lax.broadcasted_iota(jnp.int32, (1, LANES), 1)
    tiles = [_rot_groups(jnp.where(lane < LANES // 2, o_ref[0, t], 0.0), t) for t in range(PACK_STEPS)]
    for h, acc in enumerate(_gather_rotated(tiles)[:RW_HEADS]):
        mt_s[pl.ds(h, RW_HEAD_DIM, stride=RW_HEADS), :] = _rot_groups(acc, -h)
    _step_lanes_to_rows(mt_s, m_s, out_ref)


def _pack_scratch(n):
    return [pltpu.VMEM((n, RW_WIDTH // LANES, LANES, LANES), F32), pltpu.VMEM((n, RW_WIDTH, LANES), F32)]


def _scan_pack(arrs):
    b, t, _ = arrs[0].shape
    g = b // SCAN_NB
    rows = pl.BlockSpec((SCAN_NB, PACK_STEPS, RW_WIDTH), lambda gi, j: (gi, j, 0))
    kspec = pl.BlockSpec((1, PACK_STEPS, SCAN_KH, LANES), lambda gi, j: (gi, j, 0, 0))
    vspec = pl.BlockSpec((1, PACK_STEPS, RW_HEAD_DIM, LANES), lambda gi, j: (gi, j, 0, 0))
    return pl.pallas_call(
        _scan_pack_body,
        grid=(g, t // PACK_STEPS),
        in_specs=[rows] * 6,
        out_specs=[kspec] * 5 + [vspec],
        out_shape=[jax.ShapeDtypeStruct((g, t, SCAN_KH, LANES), F32)] * 5
        + [jax.ShapeDtypeStruct((g, t, RW_HEAD_DIM, LANES), F32)],
        scratch_shapes=_pack_scratch(6),
        compiler_params=_cparams(("arbitrary", "arbitrary")),
        name="scan_pack",
    )(*arrs)


def _scan_unpack(o):
    g, t = o.shape[:2]
    return pl.pallas_call(
        _scan_unpack_body,
        grid=(g, t // PACK_STEPS),
        in_specs=[pl.BlockSpec((1, PACK_STEPS, RW_HEAD_DIM, LANES), lambda gi, j: (gi, j, 0, 0))],
        out_specs=pl.BlockSpec((SCAN_NB, PACK_STEPS, RW_WIDTH), lambda gi, j: (gi, j, 0)),
        out_shape=jax.ShapeDtypeStruct((g * SCAN_NB, t, RW_WIDTH), F32),
        scratch_shapes=_pack_scratch(1),
        compiler_params=_cparams(("arbitrary", "arbitrary")),
        name="scan_unpack",
    )(o)


def _seq_to_lanes(x):
    b, t, c = x.shape
    return x.reshape(b // SCAN_NB, SCAN_NB, t, c).transpose(0, 2, 3, 1)


def _scan_layouts(nkk, w, bb, k2, r, v, wkv0):
    b, t, _ = v.shape
    g = b // SCAN_NB
    if t % PACK_STEPS == 0:
        *rows, vc = _scan_pack([nkk, w, bb, k2, r, v])
    else:
        rows = [_seq_to_lanes(a).reshape(g, t, SCAN_KH, LANES) for a in (nkk, w, bb, k2, r)]
        vc = _seq_to_lanes(v).reshape(g, t, RW_HEAD_DIM, LANES // 2)
        vc = jnp.concatenate([vc, vc], axis=-1)
    if wkv0 is None:
        s0 = jnp.zeros((g, SCAN_KH, RW_HEAD_DIM, LANES), F32)
    else:
        s0 = wkv0.reshape(g, SCAN_NB, RW_HEADS, RW_HEAD_DIM, SCAN_KH, 2)
        s0 = s0.transpose(0, 4, 3, 5, 2, 1).reshape(g, SCAN_KH, RW_HEAD_DIM, LANES)
    return rows, vc, s0


def _scan_unlayout(o, sf):
    g, t = o.shape[:2]
    if t % PACK_STEPS == 0:
        o = _scan_unpack(o)
    else:
        o = o[..., :LANES // 2].reshape(g, t, RW_WIDTH, SCAN_NB)
        o = o.transpose(0, 3, 1, 2).reshape(g * SCAN_NB, t, RW_WIDTH)
    sf = sf.reshape(g, SCAN_KH, RW_HEAD_DIM, 2, RW_HEADS, SCAN_NB)
    sf = sf.transpose(0, 5, 4, 2, 1, 3).reshape(g * SCAN_NB, RW_HEADS, RW_HEAD_DIM, RW_HEAD_DIM)
    return o, sf


def _hgrn_body(q_ref, f_ref, i_ref, g_ref, lbl_ref, ng_ref, st_ref, o_ref, so_ref,
               qin_s, kin_s, kout_s, d_s, *, chunk, per_seq, layer):
    rows = q_ref.shape[0]
    lg = lbl_ref[...]
    e = jnp.exp(lg - jnp.max(lg, axis=0, keepdims=True))
    sm = e / jnp.sum(e, axis=0, keepdims=True)
    lb = jnp.sum(sm[0:layer + 1], axis=0, keepdims=True)
    f = lb + (1.0 - lb) * _sigmoid(f_ref[...])
    logf = jnp.log(f)
    kf = 1.0 - f
    pos = lax.broadcasted_iota(jnp.int32, (rows, LANES), 0) % chunk
    cum = logf
    suf = logf
    s = 1
    while s < chunk:
        cum = cum + jnp.where(pos >= s, pltpu.roll(cum, s, axis=0), 0.0)
        suf = suf + jnp.where(pos < chunk - s, pltpu.roll(suf, rows - s, axis=0), 0.0)
        s *= 2
    qin_s[...] = q_ref[...] * jnp.exp(cum)
    kin_s[...] = kf * jnp.exp(-cum)
    kout_s[...] = kf * jnp.exp(suf - logf)
    d_s[...] = jnp.exp(cum + suf - logf)
    so_ref[...] = st_ref[...]

    nch = LANES // chunk
    ri = lax.broadcasted_iota(jnp.int32, (LANES, LANES), 0)
    ci = lax.broadcasted_iota(jnp.int32, (LANES, LANES), 1)
    causal = (ri // chunk == ci // chunk) & (ci <= ri)
    ng = ng_ref[...]

    def tile(ti, carry):
        r0 = pl.multiple_of(ti * LANES, LANES)
        qi = qin_s[pl.ds(r0, LANES), :]
        ki = kin_s[pl.ds(r0, LANES), :]
        ko = kout_s[pl.ds(r0, LANES), :]
        dd = d_s[pl.ds(r0, LANES), :]
        vv = i_ref[pl.ds(r0, LANES), :]
        qb = qi.astype(BF16)
        vb = vv.astype(BF16)
        att = jnp.where(causal, _dot_nt(qb, ki.astype(BF16)), 0.0)
        o = _dot(att.astype(BF16), vb)
        kot = ko.T
        if per_seq:
            sall = jnp.concatenate([so_ref[c] for c in range(nch)], axis=1).astype(BF16)
            big = _dot(qb, sall)
            inter = jnp.zeros((LANES, LANES), F32)
            for c in range(nch):
                inter = inter + jnp.where(ri // chunk == c, big[:, c * LANES:(c + 1) * LANES], 0.0)
            o = o + inter
        dcols = [jnp.broadcast_to(dd[c * chunk:c * chunk + 1, :], (LANES, LANES)).T for c in range(nch)]
        kvs = [_dot(jnp.where(ci // chunk == c, kot, 0.0).astype(BF16), vb) for c in range(nch)]
        if per_seq:
            for c in range(nch):
                so_ref[c] = so_ref[c] * dcols[c] + kvs[c]
        else:
            st = so_ref[0]
            parts = []
            for c in range(nch):
                parts.append(_dot(qb[c * chunk:(c + 1) * chunk], st.astype(BF16)))
                st = st * dcols[c] + kvs[c]
            so_ref[0] = st
            o = o + jnp.concatenate(parts, axis=0)
        o = o * lax.rsqrt(jnp.mean(o * o, axis=-1, keepdims=True) + NORM_EPS) * ng
        gate = g_ref[pl.ds(r0, LANES), :]
        o_ref[pl.ds(r0, LANES), :] = o * (gate * _sigmoid(gate))
        return carry

    ntile = rows // LANES
    lax.fori_loop(0, ntile, tile, 0, unroll=4 if ntile % 4 == 0 else 1)


def _hgrn(z2, lb_logits, norm_g, st0, *, b, t, layer):
    chunk = min(HG_CHUNK, t)
    per_seq = t < LANES
    if per_seq:
        rows = LANES
        ns = LANES // t
    else:
        rows = t
        ns = 1
    nblk = (b * t) // rows
    c0 = SEG // LANES

    def col(off):
        return pl.BlockSpec((rows, LANES), lambda g, h: (g, c0 + off * HG_HEADS + h))

    st_spec = pl.BlockSpec((ns, None, HG_DK, HG_DV), lambda g, h: (g, h, 0, 0))
    nd = lb_logits.shape[0]
    scr = pltpu.VMEM((rows, LANES), F32)
    return pl.pallas_call(
        functools.partial(_hgrn_body, chunk=chunk, per_seq=per_seq, layer=layer),
        grid=(nblk, HG_HEADS),
        in_specs=[col(0), col(1), col(2), col(3),
                  pl.BlockSpec((nd, LANES), lambda g, h: (0, h)),
                  pl.BlockSpec((1, LANES), lambda g, h: (0, 0)),
                  st_spec],
        out_specs=[pl.BlockSpec((rows, LANES), lambda g, h: (g, h)), st_spec],
        out_shape=[jax.ShapeDtypeStruct((b * t, HG_HEADS * HG_DV), F32),
                   jax.ShapeDtypeStruct(st0.shape, F32)],
        scratch_shapes=[scr, scr, scr, scr],
        compiler_params=_cparams(("arbitrary", "arbitrary")),
        name="hgrn",
    )(z2, z2, z2, z2, lb_logits, norm_g, st0)


def _merge_body(x_ref, o_ref, bon_ref, g_ref, ob_ref, gate_ref, lnw_ref, lnb_ref,
                wa_ref, wb_ref, wo_ref, nf_ref, wq_ref, keys_ref,
                x1_o, xn_o, st_o):
    o = o_ref[...]
    d = o - _headsum(o) * (1.0 / RW_HEAD_DIM)
    var = _headsum(d * d) * (1.0 / RW_HEAD_DIM)
    oa = d * lax.rsqrt(var + RW_GN_EPS) * lnw_ref[...] + lnb_ref[...]
    oa = (oa + bon_ref[...]) * g_ref[...]
    ya = _dot(oa.astype(BF16), wa_ref[...])
    yb = _dot(ob_ref[...].astype(BF16), wb_ref[...])
    gate = gate_ref[...]
    merged = _sigmoid(gate[:, 0:D_MODEL]) * ya + _sigmoid(gate[:, D_MODEL:]) * yb
    x1 = x_ref[...] + _dot(merged.astype(BF16), wo_ref[...])
    x1_o[...] = x1
    xn = _rms(x1, nf_ref[...]).astype(BF16)
    xn_o[...] = xn
    for c in range(0, PEER_HEADS * PEER_QDIM, 512):
        q = _dot(xn, wq_ref[:, c:c + 512]).astype(BF16)
        for s in range(4):
            hc = c // LANES + s
            st_o[hc] = _dot_nt(keys_ref[hc], q[:, s * LANES:(s + 1) * LANES])


def _merge(x, o_raw, bonus, g, o_b, z2, p):
    n = x.shape[0]
    tm = min(256, n)
    nhc = 2 * PEER_HEADS
    r512 = pl.BlockSpec((tm, RW_WIDTH), lambda i: (i, 0))
    r1024 = pl.BlockSpec((tm, D_MODEL), lambda i: (i, 0))

    def full(a):
        nd = a.ndim
        return pl.BlockSpec(a.shape, lambda i: (0,) * nd)

    ws = [p["ln_w"], p["ln_b"], p["w_up_a"], p["w_up_b"], p["w_out"], p["norm_ffn_g"], p["w_q"], p["keys"]]
    return pl.pallas_call(
        _merge_body,
        grid=(n // tm,),
        in_specs=[r1024, r512, r512, r512, r512,
                  pl.BlockSpec((tm, SEG), lambda i: (i, 2))] + [full(a) for a in ws],
        out_specs=[r1024, r1024, pl.BlockSpec((nhc, PEER_KEYS, tm), lambda i: (0, 0, i))],
        out_shape=[jax.ShapeDtypeStruct((n, D_MODEL), F32),
                   jax.ShapeDtypeStruct((n, D_MODEL), BF16),
                   jax.ShapeDtypeStruct((nhc, PEER_KEYS, n), F32)],
        compiler_params=_cparams(("arbitrary",)),
        name="merge",
    )(x, o_raw, bonus, g, o_b, z2, *ws)


def _top16(xs, payloads):
    nrow = xs[0].shape[0]
    rowi = lax.broadcasted_iota(jnp.int32, xs[0].shape, 0).astype(F32)
    slot = lax.broadcasted_iota(jnp.int32, (PEER_TOPK, LANES), 0)
    zero = jnp.zeros((PEER_TOPK, LANES), F32)

    sub = 8
    rowc = [rowi[i:i + sub] for i in range(0, nrow, sub)]

    def argmax_rows(x):
        pairs = [(x[i:i + sub], rc) for i, rc in zip(range(0, nrow, sub), rowc)]
        while len(pairs) > 1:
            nxt = []
            for j in range(0, len(pairs) - 1, 2):
                (av, ai), (bv, bi) = pairs[j], pairs[j + 1]
                keep = av >= bv
                nxt.append((jnp.where(keep, av, bv), jnp.where(keep, ai, bi)))
            if len(pairs) % 2:
                nxt.append(pairs[-1])
            pairs = nxt
        v, i = pairs[0]
        m = jnp.max(v, axis=0, keepdims=True)
        return m, jnp.min(jnp.where(v == m, i, float(nrow)), axis=0, keepdims=True)

    def it(r, carry):
        here = slot == r
        new = []
        for (x, vals, outs), pay in zip(carry, payloads):
            m, idx = argmax_rows(x)
            vals = jnp.where(here, m, vals)
            picked = (idx,) if pay is None else pay(idx)
            outs = tuple(jnp.where(here, p, o) for p, o in zip(picked, outs))
            new.append((jnp.where(rowi == idx, -jnp.inf, x), vals, outs))
        return tuple(new)

    init = tuple((x, zero, (zero,) if pay is None else (zero, zero)) for x, pay in zip(xs, payloads))
    return [(vals, outs) for _, vals, outs in lax.fori_loop(0, PEER_TOPK, it, init)]


def _pair_candidates(v1, i1, v2, i2):
    sub = 8
    row8 = lax.broadcasted_iota(jnp.int32, (sub, LANES), 0)
    cand = [v1[0:1] + v2]
    for k1 in range(1, sub):
        cand.append(jnp.where(row8 < PEER_TOPK // (k1 + 1), v1[k1:k1 + 1] + v2[0:sub], -jnp.inf))
    cand.append(v1[sub:] + v2[0:1])
    tail = PEER_TOPK + sub * (sub - 1)
    slot = lax.broadcasted_iota(jnp.int32, (PEER_TOPK, LANES), 0)

    def keys_of(idx):
        r = idx.astype(jnp.int32)
        mid = r - PEER_TOPK
        k1 = jnp.where(r < PEER_TOPK, 0, jnp.where(r < tail, 1 + (mid >> 3), r - tail + sub))
        k2 = jnp.where(r < PEER_TOPK, r, jnp.where(r < tail, mid & (sub - 1), 0))
        return (jnp.sum(jnp.where(slot == k1, i1, 0.0), axis=0, keepdims=True),
                jnp.sum(jnp.where(slot == k2, i2, 0.0), axis=0, keepdims=True))

    return jnp.concatenate(cand, axis=0), keys_of


TOPK_HEADS_PER_ROUND = 4


def _topk_body(st_ref, a_o, b_o, g_o, a_s, b_s, g_s):
    nh = TOPK_HEADS_PER_ROUND

    def group(gi, carry):
        halves = _top16([st_ref[2 * nh * gi + i] for i in range(2 * nh)], [None] * (2 * nh))
        cands = [_pair_candidates(halves[2 * i][0], halves[2 * i][1][0], halves[2 * i + 1][0], halves[2 * i + 1][1][0])
                 for i in range(nh)]
        picks = _top16([c[0] for c in cands], [c[1] for c in cands])
        for i, (cv, (ea, eb)) in enumerate(picks):
            ex = jnp.exp(cv - cv[0:1])
            r0 = pl.multiple_of((gi * nh + i) * PEER_TOPK, PEER_TOPK)
            a_s[pl.ds(r0, PEER_TOPK), :] = ea
            b_s[pl.ds(r0, PEER_TOPK), :] = eb
            g_s[pl.ds(r0, PEER_TOPK), :] = ex / jnp.sum(ex, axis=0, keepdims=True)
        return carry

    lax.fori_loop(0, PEER_HEADS // nh, group, 0)
    a_o[...] = a_s[...].T
    b_o[...] = b_s[...].T
    g_o[...] = g_s[...].T


def _topk(st):
    nhc, nk, n = st.shape
    out = jax.ShapeDtypeStruct((n, LANES), F32)
    ospec = pl.BlockSpec((LANES, LANES), lambda i: (i, 0))
    scr = pltpu.VMEM((LANES, LANES), F32)
    return pl.pallas_call(
        _topk_body,
        grid=(n // LANES,),
        in_specs=[pl.BlockSpec((nhc, nk, LANES), lambda i: (0, 0, i))],
        out_specs=[ospec] * 3,
        out_shape=[out] * 3,
        scratch_shapes=[scr] * 3,
        compiler_params=_cparams(("arbitrary",)),
        name="topk",
    )(st)


def _peer_body(x1_ref, xn_ref, a_ref, b_ref, gt_ref, u_ref, v_ref, nf_ref, y_ref, gate_s, acc_s, *, tb):
    c = pl.program_id(1)

    @pl.when(c == 0)
    def _():
        acc_s[...] = jnp.zeros_like(acc_s)
        sub = lax.broadcasted_iota(jnp.int32, (PEER_KEYS, LANES), 0).astype(F32)

        def tok(t, carry):
            ar = a_ref[pl.ds(t, 1), :]
            br = b_ref[pl.ds(t, 1), :]
            gr = gt_ref[pl.ds(t, 1), :]
            oa = jnp.where(sub == ar, 1.0, 0.0).astype(BF16)
            ob = jnp.where(sub == br, gr, 0.0).astype(BF16)
            gate_s[pl.ds(pl.multiple_of(t * PEER_KEYS, PEER_KEYS), PEER_KEYS), :] = _dot_nt(oa, ob)
            return carry

        lax.fori_loop(0, tb, tok, 0, unroll=64)

    xn = xn_ref[...]
    parts = []
    for k in range(PEER_STEP // PEER_CHUNK):
        u = _dot_nt(xn, u_ref[k * PEER_CHUNK:(k + 1) * PEER_CHUNK, :])
        act = 0.5 * u * (1.0 + lax.erf(u * (2.0 ** -0.5)))
        row = c * (PEER_STEP // PEER_KEYS) + 2 * k
        g0 = gate_s[pl.ds(row, tb, stride=PEER_KEYS), :]
        g1 = gate_s[pl.ds(row + 1, tb, stride=PEER_KEYS), :]
        parts.append((jnp.concatenate([g0, g1], axis=1) * act).astype(BF16))
    acc_s[...] += _dot(jnp.concatenate(parts, axis=1), v_ref[...])

    @pl.when(c == pl.num_programs(1) - 1)
    def _():
        y_ref[...] = _rms(x1_ref[...] + acc_s[...], nf_ref[...])


def _peer(x1, xn, a_idx, b_idx, gates, u_tab, v_tab, norm_g):
    n = x1.shape[0]
    tb = min(PEER_TOKENS, n)
    ne = u_tab.shape[0]
    row = lambda i, c: (i, 0)
    return pl.pallas_call(
        functools.partial(_peer_body, tb=tb),
        grid=(n // tb, ne // PEER_STEP),
        in_specs=[pl.BlockSpec((tb, D_MODEL), row),
                  pl.BlockSpec((tb, D_MODEL), row),
                  pl.BlockSpec((tb, LANES), row),
                  pl.BlockSpec((tb, LANES), row),
                  pl.BlockSpec((tb, LANES), row),
                  pl.BlockSpec((PEER_STEP, D_MODEL), lambda i, c: (c, 0)),
                  pl.BlockSpec((PEER_STEP, D_MODEL), lambda i, c: (c, 0)),
                  pl.BlockSpec((1, D_MODEL), lambda i, c: (0, 0))],
        out_specs=pl.BlockSpec((tb, D_MODEL), row),
        out_shape=jax.ShapeDtypeStruct((n, D_MODEL), F32),
        scratch_shapes=[pltpu.VMEM((tb * PEER_KEYS, LANES), F32),
                        pltpu.VMEM((tb, D_MODEL), F32)],
        compiler_params=_cparams(("arbitrary", "arbitrary")),
        name="peer",
    )(x1, xn, a_idx, b_idx, gates, u_tab, v_tab, norm_g)


def _pad_rows(a, n):
    return jnp.concatenate([a, jnp.zeros((n - a.shape[0],) + a.shape[1:], a.dtype)], axis=0)


def _layer_params(l, w_in, rw_mu, rw_w0, rw_w2, rw_a0, rw_a2, rw_g2, rw_k_k, rw_k_a, rw_r_k, rw_ln_w,
                  rw_ln_b, w_up_a, w_up_b, w_out, norm_ffn_g, peer_w_q, peer_keys):
    hm = jnp.asarray(_HM)
    r0, k0, v0 = 0, RW_WIDTH, 2 * RW_WIDTH
    d0 = 3 * RW_WIDTH
    a0 = d0 + RW_DECAY_RANK
    g0 = a0 + RW_AAA_RANK

    def rw_layout(m):
        z64 = jnp.zeros(m.shape[:-1] + (64,), m.dtype)
        zend = jnp.zeros(m.shape[:-1] + (SEG - RW_PAD,), m.dtype)
        return jnp.concatenate([m[..., r0:k0][..., hm], m[..., k0:v0][..., hm], m[..., v0:d0][..., hm],
                                m[..., d0:a0], z64, m[..., a0:g0], z64, m[..., g0:RW_COLS], zend], axis=-1)

    w = w_in[l]
    w_all = jnp.concatenate([rw_layout(w[:, :RW_COLS]), w[:, RW_COLS:]], axis=1).astype(BF16)
    row = lambda a: a.reshape(1, -1)
    return {
        "w_all": w_all,
        "mu": rw_layout(rw_mu[l]).reshape(1, 1, SEG),
        "w0": row(rw_w0[l][hm]),
        "w2": _pad_rows(rw_w2[l][:, hm], LANES).astype(BF16),
        "a0": row(rw_a0[l][hm]),
        "a2": _pad_rows(rw_a2[l][:, hm], LANES).astype(BF16),
        "g2": rw_g2[l][:, hm].astype(BF16),
        "k_k": row(rw_k_k[l][hm]),
        "k_a": row(rw_k_a[l][hm]),
        "r_k": row(rw_r_k[l].reshape(-1)[hm]),
        "ln_w": row(rw_ln_w[l][hm]),
        "ln_b": row(rw_ln_b[l][hm]),
        "w_up_a": w_up_a[l][hm, :].astype(BF16),
        "w_up_b": w_up_b[l].astype(BF16),
        "w_out": w_out[l].astype(BF16),
        "norm_ffn_g": row(norm_ffn_g[l]),
        "w_q": peer_w_q[l].astype(BF16),
        "keys": peer_keys[l].reshape(2 * PEER_HEADS, PEER_KEYS, PEER_QDIM // 2).astype(BF16),
    }


def _layer(l, x, shift0, wkv0, hg0, p, norm_mix_g, hg_lb_logits, hg_norm_g, u_tab, v_tab, out_norm_g):
    b, t, _ = x.shape
    n = b * t
    x2 = x.reshape(n, D_MODEL)
    gmix = norm_mix_g.reshape(1, D_MODEL)
    z2 = _proj(x2, gmix, p["w_all"], do_norm=True, nseg=3)
    new_shift = _rms_rows(x[:, -1], gmix)
    if shift0 is None:
        zfirst = jnp.zeros((b, 1, SEG), F32)
        h0 = jnp.zeros((b, HG_HEADS, HG_DK, HG_DV), F32)
    else:
        zfirst = _proj(shift0, gmix, p["w_all"], do_norm=False, nseg=1).reshape(b, 1, SEG)
        h0 = hg0
    nkk, w, bb, k2, r, v, g, bonus = _rwkv_pre(z2.reshape(b, t, 3 * SEG), zfirst, p)
    o_raw, new_wkv = _scan_unlayout(*_rwkv_scan(*_scan_layouts(nkk, w, bb, k2, r, v, wkv0)))
    o_b, new_hg = _hgrn(z2, hg_lb_logits, hg_norm_g.reshape(1, HG_DV), h0, b=b, t=t, layer=l)
    flat = lambda a: a.reshape(n, RW_WIDTH)
    x1, xn, st = _merge(x2, flat(o_raw), flat(bonus), flat(g), o_b, z2, p)
    a_idx, b_idx, gates = _topk(st)
    y = _peer(x1, xn, a_idx, b_idx, gates, u_tab, v_tab, out_norm_g.reshape(1, D_MODEL))
    return y.reshape(b, t, D_MODEL), new_shift, new_wkv, new_hg


def kernel(x_prompt, x_sample, state_rwkv_shift, state_rwkv_wkv, state_hgrn, norm_mix_g, w_in, rw_mu, rw_w0,
           rw_w2, rw_a0, rw_a2, rw_g2, rw_k_k, rw_k_a, rw_r_k, rw_ln_w, rw_ln_b, hg_lb_logits, hg_norm_g,
           w_up_a, w_up_b, w_out, norm_ffn_g, peer_w_q, peer_keys, peer_u, peer_v, norm_final_g):
    depth = w_in.shape[0]
    assert depth == 1, "the final rmsnorm is fused into the last layer's expert kernel"
    outs = []
    for l in range(depth):
        p = _layer_params(l, w_in, rw_mu, rw_w0, rw_w2, rw_a0, rw_a2, rw_g2, rw_k_k, rw_k_a, rw_r_k,
                          rw_ln_w, rw_ln_b, w_up_a, w_up_b, w_out, norm_ffn_g, peer_w_q, peer_keys)
        u_tab = peer_u[l].astype(BF16)
        v_tab = peer_v[l].astype(BF16)
        common = (p, norm_mix_g[l], hg_lb_logits, hg_norm_g[l], u_tab, v_tab, norm_final_g)
        yp, p_shift, p_wkv, p_hg = _layer(l, x_prompt, None, None, None, *common)
        ys, s_shift, s_wkv, s_hg = _layer(l, x_sample, state_rwkv_shift[l], state_rwkv_wkv[l],
                                          state_hgrn[l], *common)
        outs = (yp, ys, p_shift[None], p_wkv[None], p_hg[None], s_shift[None], s_wkv[None], s_hg[None])
    return outs
```

```python
import functools

import numpy as np
import jax
import jax.numpy as jnp
from jax import lax
from jax.experimental import pallas as pl
from jax.experimental.pallas import tpu as pltpu

F32 = jnp.float32
BF16 = jnp.bfloat16

D_MODEL = 1024
RW_HEADS = 8
RW_HEAD_DIM = 64
RW_WIDTH = 512
RW_DECAY_RANK = 64
RW_AAA_RANK = 64
RW_GATE_RANK = 128
RW_COLS = 1792
RW_GN_EPS = 64e-5
HG_HEADS = 4
HG_DK = 128
HG_DV = 128
HG_COLS = 2048
HG_CHUNK = 32
GATE_COLS = 2048
PEER_KEYS = 128
PEER_HEADS = 8
PEER_TOPK = 16
PEER_QDIM = 256
NORM_EPS = 1e-6

LANES = 128
SEG = 2048
RW_PAD = 1920
SCAN_NB = 8
SCAN_KH = RW_HEAD_DIM // (LANES // (RW_HEADS * SCAN_NB))
SCAN_ACCS = 4
PEER_CHUNK = 256
PEER_STEP = 1024
PEER_TOKENS = 512
VMEM_LIMIT = 60000 * 1024

_HM = (np.arange(RW_WIDTH) % RW_HEADS) * RW_HEAD_DIM + np.arange(RW_WIDTH) // RW_HEADS


def _cparams(sem):
    return pltpu.CompilerParams(dimension_semantics=sem, vmem_limit_bytes=VMEM_LIMIT)


def _sigmoid(x):
    return 1.0 / (1.0 + jnp.exp(-x))


def _rms(x, g):
    return x * lax.rsqrt(jnp.mean(x * x, axis=-1, keepdims=True) + NORM_EPS) * g


def _headsum(x):
    s = x[:, 0:128] + x[:, 128:256] + x[:, 256:384] + x[:, 384:512]
    for sh in (64, 32, 16, 8):
        s = s + pltpu.roll(s, sh, axis=1)
    return jnp.concatenate([s, s, s, s], axis=1)


def _dot(a, b):
    return jnp.dot(a, b, preferred_element_type=F32)


def _dot_nt(a, b):
    return lax.dot_general(a, b, (((1,), (1,)), ((), ())), preferred_element_type=F32)


def _proj_body(x_ref, g_ref, w_ref, o_ref, *, do_norm):
    x = x_ref[...]
    if do_norm:
        x = _rms(x, g_ref[...])
    xb = x.astype(BF16)
    for c in range(0, SEG, 512):
        o_ref[:, c:c + 512] = _dot(xb, w_ref[:, c:c + 512])


def _proj(x, g, w, *, do_norm, nseg):
    n = x.shape[0]
    tm = min(512, n)
    return pl.pallas_call(
        functools.partial(_proj_body, do_norm=do_norm),
        grid=(nseg, n // tm),
        in_specs=[pl.BlockSpec((tm, D_MODEL), lambda c, i: (i, 0)),
                  pl.BlockSpec((1, D_MODEL), lambda c, i: (0, 0)),
                  pl.BlockSpec((D_MODEL, SEG), lambda c, i: (0, c))],
        out_specs=pl.BlockSpec((tm, SEG), lambda c, i: (i, c)),
        out_shape=jax.ShapeDtypeStruct((n, nseg * SEG), F32),
        compiler_params=_cparams(("arbitrary", "arbitrary")),
        name="proj",
    )(x, g, w)


def _rms_rows_body(x_ref, g_ref, o_ref):
    o_ref[...] = _rms(x_ref[...], g_ref[...])


def _rms_rows(x, g):
    return pl.pallas_call(
        _rms_rows_body,
        out_shape=jax.ShapeDtypeStruct(x.shape, F32),
        name="rms_rows",
    )(x, g)


def _rwkv_pre_body(z_ref, zp_ref, f_ref, mu_ref, w0_ref, w2_ref, a0_ref, a2_ref, g2_ref,
                   kk_ref, ka_ref, rk_ref,
                   nkk_o, w_o, bb_o, k2_o, r_o, v_o, g_o, bon_o):
    j = pl.program_id(1)
    z = z_ref[...]
    bt, tt, c = z.shape
    pos = lax.broadcasted_iota(jnp.int32, z.shape, 1)
    first = jnp.where(j == 0, f_ref[...], zp_ref[:, 7:8, :])
    zprev = jnp.where(pos == 0, first, pltpu.roll(z, 1, axis=1))
    zs = (z + (zprev - z) * mu_ref[...]).reshape(bt * tt, c)
    r = zs[:, 0:512]
    k = zs[:, 512:1024]
    v = zs[:, 1024:1536]
    dl = zs[:, 1536:1664]
    al = zs[:, 1664:1792]
    gl = zs[:, 1792:1920]
    y = w0_ref[...] + _dot(jnp.tanh(dl).astype(BF16), w2_ref[...])
    softplus_neg = jnp.maximum(-y, 0.0) + jnp.log(1.0 + jnp.exp(-jnp.abs(y)))
    decay = jnp.exp(-jnp.exp(-softplus_neg - 0.5))
    a = _sigmoid(a0_ref[...] + _dot(al.astype(BF16), a2_ref[...]))
    g = _dot(_sigmoid(gl).astype(BF16), g2_ref[...])
    kk = k * kk_ref[...]
    kk = kk / jnp.maximum(jnp.sqrt(_headsum(kk * kk)), 1e-12)
    k2 = k * (1.0 + (a - 1.0) * ka_ref[...])
    bonus = _headsum(r * k2 * rk_ref[...]) * v
    for ref, val in ((nkk_o, -kk), (w_o, decay), (bb_o, kk * a), (k2_o, k2), (r_o, r), (v_o, v), (g_o, g),
                     (bon_o, bonus)):
        ref[...] = val.reshape(ref.shape)


def _rwkv_pre(z3, zfirst, p):
    b, t, _ = z3.shape
    if t >= 256:
        bt, tt = 1, 256
    else:
        bt, tt = min(b, 256 // t), t
    tpb = tt // 8
    row = lambda bi, j: (bi, j, 0)
    const2 = lambda bi, j: (0, 0)
    vec = pl.BlockSpec((1, RW_WIDTH), const2)
    low = pl.BlockSpec((LANES, RW_WIDTH), const2)
    out_spec = pl.BlockSpec((bt, tt, RW_WIDTH), row)
    out_shape = jax.ShapeDtypeStruct((b, t, RW_WIDTH), F32)
    return pl.pallas_call(
        _rwkv_pre_body,
        grid=(b // bt, t // tt),
        in_specs=[pl.BlockSpec((bt, tt, SEG), row),
                  pl.BlockSpec((bt, 8, SEG), lambda bi, j: (bi, jnp.maximum(j * tpb - 1, 0), 0)),
                  pl.BlockSpec((bt, 1, SEG), lambda bi, j: (bi, 0, 0)),
                  pl.BlockSpec((1, 1, SEG), lambda bi, j: (0, 0, 0)),
                  vec, low, vec, low, low, vec, vec, vec],
        out_specs=[out_spec] * 8,
        out_shape=[out_shape] * 8,
        compiler_params=_cparams(("arbitrary", "arbitrary")),
        name="rwkv_pre",
    )(z3, z3, zfirst, p["mu"], p["w0"], p["w2"], p["a0"], p["a2"], p["g2"],
      p["k_k"], p["k_a"], p["r_k"])


def _rwkv_scan_body(nkk_ref, w_ref, bb_ref, k2_ref, r_ref, vc_ref, s0_ref, o_ref, sf_ref, s_scr, *, tc):
    j = pl.program_id(1)

    @pl.when(j == 0)
    def _():
        s_scr[...] = s0_ref[0]

    def halfsum(parts):
        while len(parts) > 1:
            parts = [parts[i] + parts[i + 1] for i in range(0, len(parts), 2)]
        return parts[0] + pltpu.roll(parts[0], LANES // 2, axis=1)

    def step(t, carry):
        nkk = nkk_ref[0, t]
        wt = w_ref[0, t]
        bbt = bb_ref[0, t]
        k2t = k2_ref[0, t]
        rt = r_ref[0, t]
        vcol = vc_ref[0, t]
        accs = [None] * SCAN_ACCS
        for kh in range(SCAN_KH):
            p = s_scr[kh] * nkk[kh:kh + 1]
            accs[kh % SCAN_ACCS] = p if kh < SCAN_ACCS else accs[kh % SCAN_ACCS] + p
        sa = halfsum(accs)
        accs = [None] * SCAN_ACCS
        for kh in range(SCAN_KH):
            s = s_scr[kh] * wt[kh:kh + 1] + sa * bbt[kh:kh + 1] + vcol * k2t[kh:kh + 1]
            s_scr[kh] = s
            q = s * rt[kh:kh + 1]
            accs[kh % SCAN_ACCS] = q if kh < SCAN_ACCS else accs[kh % SCAN_ACCS] + q
        o_ref[0, t] = halfsum(accs)
        return carry

    lax.fori_loop(0, tc, step, 0)

    @pl.when(j == pl.num_programs(1) - 1)
    def _():
        sf_ref[0] = s_scr[...]


def _rwkv_scan(rows, vcols, s0):
    g, t = vcols.shape[:2]
    tc = min(64, t)
    row = pl.BlockSpec((1, tc, SCAN_KH, LANES), lambda gi, j: (gi, j, 0, 0))
    col = pl.BlockSpec((1, tc, RW_HEAD_DIM, LANES), lambda gi, j: (gi, j, 0, 0))
    st = pl.BlockSpec((1, SCAN_KH, RW_HEAD_DIM, LANES), lambda gi, j: (gi, 0, 0, 0))
    return pl.pallas_call(
        functools.partial(_rwkv_scan_body, tc=tc),
        grid=(g, t // tc),
        in_specs=[row] * 5 + [col, st],
        out_specs=[col, st],
        out_shape=[jax.ShapeDtypeStruct(vcols.shape, F32), jax.ShapeDtypeStruct(s0.shape, F32)],
        scratch_shapes=[pltpu.VMEM((SCAN_KH, RW_HEAD_DIM, LANES), F32)],
        compiler_params=_cparams(("arbitrary", "arbitrary")),
        name="rwkv_scan",
    )(*rows, vcols, s0)


PACK_STEPS = LANES // SCAN_NB


def _slot_rows(q, ts):
    return pl.ds((q // SCAN_NB) * ts * SCAN_NB + q % SCAN_NB, ts, stride=SCAN_NB)


def _rows_to_step_lanes(x_ref, m_s, mt_s):
    nq, ts, _ = x_ref.shape
    for j in range(RW_WIDTH // LANES):
        for q in range(nq):
            m_s[j, _slot_rows(q, ts), :] = x_ref[q, :, j * LANES:(j + 1) * LANES]
        mt_s[j * LANES:(j + 1) * LANES, :] = m_s[j].T


def _step_lanes_to_rows(mt_s, m_s, o_ref):
    nq, ts, _ = o_ref.shape
    for j in range(RW_WIDTH // LANES):
        m_s[j] = mt_s[j * LANES:(j + 1) * LANES, :].T
        for q in range(nq):
            o_ref[q, :, j * LANES:(j + 1) * LANES] = m_s[j, _slot_rows(q, ts), :]


def _rot_groups(x, groups):
    groups %= PACK_STEPS
    return pltpu.roll(x, groups * SCAN_NB, axis=1) if groups else x


def _gather_rotated(tiles):
    n = PACK_STEPS
    group = lax.broadcasted_iota(jnp.int32, (1, LANES), 1) // SCAN_NB
    lst = [tiles[(-m) % n] for m in range(n)]
    for k in range(n.bit_length() - 1):
        bit = ((group >> k) & 1) == 1
        lst = [jnp.where(bit, lst[(s - (1 << k)) % n], lst[s]) for s in range(n)]
    return lst


def _pack_one(x_ref, y_ref, m_s, mt_s, nblk, dup):
    nrow = RW_WIDTH // nblk
    ts = y_ref.shape[1]
    _rows_to_step_lanes(x_ref, m_s, mt_s)
    zero = jnp.zeros((nrow, LANES), F32)
    tiles = [_rot_groups(mt_s[pl.ds(blk, nrow, stride=nblk), :], blk) if blk < nblk else zero
             for blk in range(PACK_STEPS)]
    for s, acc in enumerate(_gather_rotated(tiles)):
        y = _rot_groups(acc, -s)
        y_ref[s // ts, s % ts] = y + pltpu.roll(y, LANES // 2, axis=1) if dup else y


def _scan_pack_body(nkk_ref, w_ref, bb_ref, k2_ref, r_ref, v_ref,
                    nkk_o, w_o, bb_o, k2_o, r_o, vc_o, m_s, mt_s):
    pairs = ((nkk_ref, nkk_o), (w_ref, w_o), (bb_ref, bb_o), (k2_ref, k2_o), (r_ref, r_o))
    for i, (x_ref, y_ref) in enumerate(pairs):
        _pack_one(x_ref, y_ref, m_s.at[i], mt_s.at[i], 2 * RW_HEADS, False)
    _pack_one(v_ref, vc_o, m_s.at[5], mt_s.at[5], RW_HEADS, True)


def _scan_unpack_body(o_ref, out_ref, m_s, mt_s):
    m_s, mt_s = m_s.at[0], mt_s.at[0]
    ts = o_ref.shape[1]
    lane = lax.broadcasted_iota(jnp.int32, (1, LANES), 1)
    tiles = [_rot_groups(jnp.where(lane < LANES // 2, o_ref[s // ts, s % ts], 0.0), s) for s in range(PACK_STEPS)]
    for h, acc in enumerate(_gather_rotated(tiles)[:RW_HEADS]):
        mt_s[pl.ds(h, RW_HEAD_DIM, stride=RW_HEADS), :] = _rot_groups(acc, -h)
    _step_lanes_to_rows(mt_s, m_s, out_ref)


def _pack_scratch(n):
    return [pltpu.VMEM((n, RW_WIDTH // LANES, LANES, LANES), F32), pltpu.VMEM((n, RW_WIDTH, LANES), F32)]


def _pack_blocking(g, t):
    ts = min(t, PACK_STEPS)
    gp = PACK_STEPS // ts
    return (ts, gp) if t % ts == 0 and PACK_STEPS % ts == 0 and g % gp == 0 else None


def _scan_pack(arrs, ts, gp):
    b, t, _ = arrs[0].shape
    g = b // SCAN_NB
    rows = pl.BlockSpec((gp * SCAN_NB, ts, RW_WIDTH), lambda gi, j: (gi, j, 0))
    kspec = pl.BlockSpec((gp, ts, SCAN_KH, LANES), lambda gi, j: (gi, j, 0, 0))
    vspec = pl.BlockSpec((gp, ts, RW_HEAD_DIM, LANES), lambda gi, j: (gi, j, 0, 0))
    return pl.pallas_call(
        _scan_pack_body,
        grid=(g // gp, t // ts),
        in_specs=[rows] * 6,
        out_specs=[kspec] * 5 + [vspec],
        out_shape=[jax.ShapeDtypeStruct((g, t, SCAN_KH, LANES), F32)] * 5
        + [jax.ShapeDtypeStruct((g, t, RW_HEAD_DIM, LANES), F32)],
        scratch_shapes=_pack_scratch(6),
        compiler_params=_cparams(("arbitrary", "arbitrary")),
        name="scan_pack",
    )(*arrs)


def _scan_unpack(o, ts, gp):
    g, t = o.shape[:2]
    return pl.pallas_call(
        _scan_unpack_body,
        grid=(g // gp, t // ts),
        in_specs=[pl.BlockSpec((gp, ts, RW_HEAD_DIM, LANES), lambda gi, j: (gi, j, 0, 0))],
        out_specs=pl.BlockSpec((gp * SCAN_NB, ts, RW_WIDTH), lambda gi, j: (gi, j, 0)),
        out_shape=jax.ShapeDtypeStruct((g * SCAN_NB, t, RW_WIDTH), F32),
        scratch_shapes=_pack_scratch(1),
        compiler_params=_cparams(("arbitrary", "arbitrary")),
        name="scan_unpack",
    )(o)


def _seq_to_lanes(x):
    b, t, c = x.shape
    return x.reshape(b // SCAN_NB, SCAN_NB, t, c).transpose(0, 2, 3, 1)


def _scan_layouts(nkk, w, bb, k2, r, v, wkv0):
    b, t, _ = v.shape
    g = b // SCAN_NB
    blocking = _pack_blocking(g, t)
    if blocking:
        *rows, vc = _scan_pack([nkk, w, bb, k2, r, v], *blocking)
    else:
        rows = [_seq_to_lanes(a).reshape(g, t, SCAN_KH, LANES) for a in (nkk, w, bb, k2, r)]
        vc = _seq_to_lanes(v).reshape(g, t, RW_HEAD_DIM, LANES // 2)
        vc = jnp.concatenate([vc, vc], axis=-1)
    if wkv0 is None:
        s0 = jnp.zeros((g, SCAN_KH, RW_HEAD_DIM, LANES), F32)
    else:
        s0 = wkv0.reshape(g, SCAN_NB, RW_HEADS, RW_HEAD_DIM, SCAN_KH, 2)
        s0 = s0.transpose(0, 4, 3, 5, 2, 1).reshape(g, SCAN_KH, RW_HEAD_DIM, LANES)
    return rows, vc, s0


def _scan_unlayout(o, sf):
    g, t = o.shape[:2]
    blocking = _pack_blocking(g, t)
    if blocking:
        o = _scan_unpack(o, *blocking)
    else:
        o = o[..., :LANES // 2].reshape(g, t, RW_WIDTH, SCAN_NB)
        o = o.transpose(0, 3, 1, 2).reshape(g * SCAN_NB, t, RW_WIDTH)
    sf = sf.reshape(g, SCAN_KH, RW_HEAD_DIM, 2, RW_HEADS, SCAN_NB)
    sf = sf.transpose(0, 5, 4, 2, 1, 3).reshape(g * SCAN_NB, RW_HEADS, RW_HEAD_DIM, RW_HEAD_DIM)
    return o, sf


def _hgrn_body(q_ref, f_ref, i_ref, g_ref, lbl_ref, ng_ref, st_ref, o_ref, so_ref,
               qin_s, kin_s, kout_s, d_s, *, chunk, per_seq, layer):
    rows = q_ref.shape[0]
    lg = lbl_ref[...]
    e = jnp.exp(lg - jnp.max(lg, axis=0, keepdims=True))
    sm = e / jnp.sum(e, axis=0, keepdims=True)
    lb = jnp.sum(sm[0:layer + 1], axis=0, keepdims=True)
    f = lb + (1.0 - lb) * _sigmoid(f_ref[...])
    logf = jnp.log(f)
    kf = 1.0 - f
    pos = lax.broadcasted_iota(jnp.int32, (rows, LANES), 0) % chunk
    cum = logf
    suf = logf
    s = 1
    while s < chunk:
        cum = cum + jnp.where(pos >= s, pltpu.roll(cum, s, axis=0), 0.0)
        suf = suf + jnp.where(pos < chunk - s, pltpu.roll(suf, rows - s, axis=0), 0.0)
        s *= 2
    qin_s[...] = q_ref[...] * jnp.exp(cum)
    kin_s[...] = kf * jnp.exp(-cum)
    kout_s[...] = kf * jnp.exp(suf - logf)
    d_s[...] = jnp.exp(cum + suf - logf)
    so_ref[...] = st_ref[...]

    nch = LANES // chunk
    ri = lax.broadcasted_iota(jnp.int32, (LANES, LANES), 0)
    ci = lax.broadcasted_iota(jnp.int32, (LANES, LANES), 1)
    causal = (ri // chunk == ci // chunk) & (ci <= ri)
    ng = ng_ref[...]

    def tile(ti, carry):
        r0 = pl.multiple_of(ti * LANES, LANES)
        qi = qin_s[pl.ds(r0, LANES), :]
        ki = kin_s[pl.ds(r0, LANES), :]
        ko = kout_s[pl.ds(r0, LANES), :]
        dd = d_s[pl.ds(r0, LANES), :]
        vv = i_ref[pl.ds(r0, LANES), :]
        qb = qi.astype(BF16)
        vb = vv.astype(BF16)
        att = jnp.where(causal, _dot_nt(qb, ki.astype(BF16)), 0.0)
        o = _dot(att.astype(BF16), vb)
        kot = ko.T
        if per_seq:
            sall = jnp.concatenate([so_ref[c] for c in range(nch)], axis=1).astype(BF16)
            big = _dot(qb, sall)
            inter = jnp.zeros((LANES, LANES), F32)
            for c in range(nch):
                inter = inter + jnp.where(ri // chunk == c, big[:, c * LANES:(c + 1) * LANES], 0.0)
            o = o + inter
        dcols = [jnp.broadcast_to(dd[c * chunk:c * chunk + 1, :], (LANES, LANES)).T for c in range(nch)]
        kvs = [_dot(jnp.where(ci // chunk == c, kot, 0.0).astype(BF16), vb) for c in range(nch)]
        if per_seq:
            for c in range(nch):
                so_ref[c] = so_ref[c] * dcols[c] + kvs[c]
        else:
            st = so_ref[0]
            parts = []
            for c in range(nch):
                parts.append(_dot(qb[c * chunk:(c + 1) * chunk], st.astype(BF16)))
                st = st * dcols[c] + kvs[c]
            so_ref[0] = st
            o = o + jnp.concatenate(parts, axis=0)
        o = o * lax.rsqrt(jnp.mean(o * o, axis=-1, keepdims=True) + NORM_EPS) * ng
        gate = g_ref[pl.ds(r0, LANES), :]
        o_ref[pl.ds(r0, LANES), :] = o * (gate * _sigmoid(gate))
        return carry

    ntile = rows // LANES
    lax.fori_loop(0, ntile, tile, 0, unroll=4 if ntile % 4 == 0 else 1)


def _hgrn(z2, lb_logits, norm_g, st0, *, b, t, layer):
    chunk = min(HG_CHUNK, t)
    per_seq = t < LANES
    if per_seq:
        rows = LANES
        ns = LANES // t
    else:
        rows = t
        ns = 1
    nblk = (b * t) // rows
    c0 = SEG // LANES

    def col(off):
        return pl.BlockSpec((rows, LANES), lambda g, h: (g, c0 + off * HG_HEADS + h))

    st_spec = pl.BlockSpec((ns, None, HG_DK, HG_DV), lambda g, h: (g, h, 0, 0))
    nd = lb_logits.shape[0]
    scr = pltpu.VMEM((rows, LANES), F32)
    return pl.pallas_call(
        functools.partial(_hgrn_body, chunk=chunk, per_seq=per_seq, layer=layer),
        grid=(nblk, HG_HEADS),
        in_specs=[col(0), col(1), col(2), col(3),
                  pl.BlockSpec((nd, LANES), lambda g, h: (0, h)),
                  pl.BlockSpec((1, LANES), lambda g, h: (0, 0)),
                  st_spec],
        out_specs=[pl.BlockSpec((rows, LANES), lambda g, h: (g, h)), st_spec],
        out_shape=[jax.ShapeDtypeStruct((b * t, HG_HEADS * HG_DV), F32),
                   jax.ShapeDtypeStruct(st0.shape, F32)],
        scratch_shapes=[scr, scr, scr, scr],
        compiler_params=_cparams(("arbitrary", "arbitrary")),
        name="hgrn",
    )(z2, z2, z2, z2, lb_logits, norm_g, st0)


def _merge_body(x_ref, o_ref, bon_ref, g_ref, ob_ref, gate_ref, lnw_ref, lnb_ref,
                wa_ref, wb_ref, wo_ref, nf_ref, wq_ref, keys_ref,
                x1_o, xn_o, st_o):
    o = o_ref[...]
    d = o - _headsum(o) * (1.0 / RW_HEAD_DIM)
    var = _headsum(d * d) * (1.0 / RW_HEAD_DIM)
    oa = d * lax.rsqrt(var + RW_GN_EPS) * lnw_ref[...] + lnb_ref[...]
    oa = (oa + bon_ref[...]) * g_ref[...]
    ya = _dot(oa.astype(BF16), wa_ref[...])
    yb = _dot(ob_ref[...].astype(BF16), wb_ref[...])
    gate = gate_ref[...]
    merged = _sigmoid(gate[:, 0:D_MODEL]) * ya + _sigmoid(gate[:, D_MODEL:]) * yb
    x1 = x_ref[...] + _dot(merged.astype(BF16), wo_ref[...])
    x1_o[...] = x1
    xn = _rms(x1, nf_ref[...]).astype(BF16)
    xn_o[...] = xn
    for c in range(0, PEER_HEADS * PEER_QDIM, 512):
        q = _dot(xn, wq_ref[:, c:c + 512]).astype(BF16)
        for s in range(4):
            hc = c // LANES + s
            st_o[hc] = _dot_nt(keys_ref[hc], q[:, s * LANES:(s + 1) * LANES])


def _merge(x, o_raw, bonus, g, o_b, z2, p):
    n = x.shape[0]
    tm = min(256, n)
    nhc = 2 * PEER_HEADS
    r512 = pl.BlockSpec((tm, RW_WIDTH), lambda i: (i, 0))
    r1024 = pl.BlockSpec((tm, D_MODEL), lambda i: (i, 0))

    def full(a):
        nd = a.ndim
        return pl.BlockSpec(a.shape, lambda i: (0,) * nd)

    ws = [p["ln_w"], p["ln_b"], p["w_up_a"], p["w_up_b"], p["w_out"], p["norm_ffn_g"], p["w_q"], p["keys"]]
    return pl.pallas_call(
        _merge_body,
        grid=(n // tm,),
        in_specs=[r1024, r512, r512, r512, r512,
                  pl.BlockSpec((tm, SEG), lambda i: (i, 2))] + [full(a) for a in ws],
        out_specs=[r1024, r1024, pl.BlockSpec((nhc, PEER_KEYS, tm), lambda i: (0, 0, i))],
        out_shape=[jax.ShapeDtypeStruct((n, D_MODEL), F32),
                   jax.ShapeDtypeStruct((n, D_MODEL), BF16),
                   jax.ShapeDtypeStruct((nhc, PEER_KEYS, n), F32)],
        compiler_params=_cparams(("arbitrary",)),
        name="merge",
    )(x, o_raw, bonus, g, o_b, z2, *ws)


def _top16(xs, payloads):
    nrow = xs[0].shape[0]
    rowi = lax.broadcasted_iota(jnp.int32, xs[0].shape, 0).astype(F32)
    slot = lax.broadcasted_iota(jnp.int32, (PEER_TOPK, LANES), 0)
    zero = jnp.zeros((PEER_TOPK, LANES), F32)

    sub = 8
    rowc = [rowi[i:i + sub] for i in range(0, nrow, sub)]

    def argmax_rows(x):
        pairs = [(x[i:i + sub], rc) for i, rc in zip(range(0, nrow, sub), rowc)]
        while len(pairs) > 1:
            nxt = []
            for j in range(0, len(pairs) - 1, 2):
                (av, ai), (bv, bi) = pairs[j], pairs[j + 1]
                keep = av >= bv
                nxt.append((jnp.where(keep, av, bv), jnp.where(keep, ai, bi)))
            if len(pairs) % 2:
                nxt.append(pairs[-1])
            pairs = nxt
        v, i = pairs[0]
        m = jnp.max(v, axis=0, keepdims=True)
        return m, jnp.min(jnp.where(v == m, i, float(nrow)), axis=0, keepdims=True)

    def it(r, carry):
        here = slot == r
        new = []
        for (x, vals, outs), pay in zip(carry, payloads):
            m, idx = argmax_rows(x)
            vals = jnp.where(here, m, vals)
            picked = (idx,) if pay is None else pay(idx)
            outs = tuple(jnp.where(here, p, o) for p, o in zip(picked, outs))
            new.append((jnp.where(rowi == idx, -jnp.inf, x), vals, outs))
        return tuple(new)

    init = tuple((x, zero, (zero,) if pay is None else (zero, zero)) for x, pay in zip(xs, payloads))
    return [(vals, outs) for _, vals, outs in lax.fori_loop(0, PEER_TOPK, it, init)]


def _pair_candidates(v1, i1, v2, i2):
    sub = 8
    row8 = lax.broadcasted_iota(jnp.int32, (sub, LANES), 0)
    cand = [v1[0:1] + v2]
    for k1 in range(1, sub):
        cand.append(jnp.where(row8 < PEER_TOPK // (k1 + 1), v1[k1:k1 + 1] + v2[0:sub], -jnp.inf))
    cand.append(v1[sub:] + v2[0:1])
    tail = PEER_TOPK + sub * (sub - 1)
    slot = lax.broadcasted_iota(jnp.int32, (PEER_TOPK, LANES), 0)

    def keys_of(idx):
        r = idx.astype(jnp.int32)
        mid = r - PEER_TOPK
        k1 = jnp.where(r < PEER_TOPK, 0, jnp.where(r < tail, 1 + (mid >> 3), r - tail + sub))
        k2 = jnp.where(r < PEER_TOPK, r, jnp.where(r < tail, mid & (sub - 1), 0))
        return (jnp.sum(jnp.where(slot == k1, i1, 0.0), axis=0, keepdims=True),
                jnp.sum(jnp.where(slot == k2, i2, 0.0), axis=0, keepdims=True))

    return jnp.concatenate(cand, axis=0), keys_of


TOPK_HEADS_PER_ROUND = 2


def _topk_body(st_ref, a_o, b_o, g_o, a_s, b_s, g_s):
    nh = TOPK_HEADS_PER_ROUND

    def group(gi, carry):
        halves = _top16([st_ref[2 * nh * gi + i] for i in range(2 * nh)], [None] * (2 * nh))
        cands = [_pair_candidates(halves[2 * i][0], halves[2 * i][1][0], halves[2 * i + 1][0], halves[2 * i + 1][1][0])
                 for i in range(nh)]
        picks = _top16([c[0] for c in cands], [c[1] for c in cands])
        for i, (cv, (ea, eb)) in enumerate(picks):
            ex = jnp.exp(cv - cv[0:1])
            r0 = pl.multiple_of((gi * nh + i) * PEER_TOPK, PEER_TOPK)
            a_s[pl.ds(r0, PEER_TOPK), :] = ea
            b_s[pl.ds(r0, PEER_TOPK), :] = eb
            g_s[pl.ds(r0, PEER_TOPK), :] = ex / jnp.sum(ex, axis=0, keepdims=True)
        return carry

    lax.fori_loop(0, PEER_HEADS // nh, group, 0)
    a_o[...] = a_s[...].T
    b_o[...] = b_s[...].T
    g_o[...] = g_s[...].T


def _topk(st):
    nhc, nk, n = st.shape
    out = jax.ShapeDtypeStruct((n, LANES), F32)
    ospec = pl.BlockSpec((LANES, LANES), lambda i: (i, 0))
    scr = pltpu.VMEM((LANES, LANES), F32)
    return pl.pallas_call(
        _topk_body,
        grid=(n // LANES,),
        in_specs=[pl.BlockSpec((nhc, nk, LANES), lambda i: (0, 0, i))],
        out_specs=[ospec] * 3,
        out_shape=[out] * 3,
        scratch_shapes=[scr] * 3,
        compiler_params=_cparams(("arbitrary",)),
        name="topk",
    )(st)


def _peer_body(x1_ref, xn_ref, a_ref, b_ref, gt_ref, u_ref, v_ref, nf_ref, y_ref, gate_s, acc_s, *, tb):
    c = pl.program_id(1)

    @pl.when(c == 0)
    def _():
        acc_s[...] = jnp.zeros_like(acc_s)
        sub = lax.broadcasted_iota(jnp.int32, (PEER_KEYS, LANES), 0).astype(F32)

        def tok(t, carry):
            ar = a_ref[pl.ds(t, 1), :]
            br = b_ref[pl.ds(t, 1), :]
            gr = gt_ref[pl.ds(t, 1), :]
            oa = jnp.where(sub == ar, 1.0, 0.0).astype(BF16)
            ob = jnp.where(sub == br, gr, 0.0).astype(BF16)
            gate_s[pl.ds(pl.multiple_of(t * PEER_KEYS, PEER_KEYS), PEER_KEYS), :] = _dot_nt(oa, ob)
            return carry

        lax.fori_loop(0, tb, tok, 0, unroll=64)

    xn = xn_ref[...]
    parts = []
    for k in range(PEER_STEP // PEER_CHUNK):
        u = _dot_nt(xn, u_ref[k * PEER_CHUNK:(k + 1) * PEER_CHUNK, :])
        act = 0.5 * u * (1.0 + lax.erf(u * (2.0 ** -0.5)))
        row = c * (PEER_STEP // PEER_KEYS) + 2 * k
        g0 = gate_s[pl.ds(row, tb, stride=PEER_KEYS), :]
        g1 = gate_s[pl.ds(row + 1, tb, stride=PEER_KEYS), :]
        parts.append((jnp.concatenate([g0, g1], axis=1) * act).astype(BF16))
    acc_s[...] += _dot(jnp.concatenate(parts, axis=1), v_ref[...])

    @pl.when(c == pl.num_programs(1) - 1)
    def _():
        y_ref[...] = _rms(x1_ref[...] + acc_s[...], nf_ref[...])


def _peer(x1, xn, a_idx, b_idx, gates, u_tab, v_tab, norm_g):
    n = x1.shape[0]
    tb = min(PEER_TOKENS, n)
    ne = u_tab.shape[0]
    row = lambda i, c: (i, 0)
    return pl.pallas_call(
        functools.partial(_peer_body, tb=tb),
        grid=(n // tb, ne // PEER_STEP),
        in_specs=[pl.BlockSpec((tb, D_MODEL), row),
                  pl.BlockSpec((tb, D_MODEL), row),
                  pl.BlockSpec((tb, LANES), row),
                  pl.BlockSpec((tb, LANES), row),
                  pl.BlockSpec((tb, LANES), row),
                  pl.BlockSpec((PEER_STEP, D_MODEL), lambda i, c: (c, 0)),
                  pl.BlockSpec((PEER_STEP, D_MODEL), lambda i, c: (c, 0)),
                  pl.BlockSpec((1, D_MODEL), lambda i, c: (0, 0))],
        out_specs=pl.BlockSpec((tb, D_MODEL), row),
        out_shape=jax.ShapeDtypeStruct((n, D_MODEL), F32),
        scratch_shapes=[pltpu.VMEM((tb * PEER_KEYS, LANES), F32),
                        pltpu.VMEM((tb, D_MODEL), F32)],
        compiler_params=_cparams(("arbitrary", "arbitrary")),
        name="peer",
    )(x1, xn, a_idx, b_idx, gates, u_tab, v_tab, norm_g)


def _pad_rows(a, n):
    return jnp.concatenate([a, jnp.zeros((n - a.shape[0],) + a.shape[1:], a.dtype)], axis=0)


def _layer_params(l, w_in, rw_mu, rw_w0, rw_w2, rw_a0, rw_a2, rw_g2, rw_k_k, rw_k_a, rw_r_k, rw_ln_w,
                  rw_ln_b, w_up_a, w_up_b, w_out, norm_ffn_g, peer_w_q, peer_keys):
    hm = jnp.asarray(_HM)
    r0, k0, v0 = 0, RW_WIDTH, 2 * RW_WIDTH
    d0 = 3 * RW_WIDTH
    a0 = d0 + RW_DECAY_RANK
    g0 = a0 + RW_AAA_RANK

    def rw_layout(m):
        z64 = jnp.zeros(m.shape[:-1] + (64,), m.dtype)
        zend = jnp.zeros(m.shape[:-1] + (SEG - RW_PAD,), m.dtype)
        return jnp.concatenate([m[..., r0:k0][..., hm], m[..., k0:v0][..., hm], m[..., v0:d0][..., hm],
                                m[..., d0:a0], z64, m[..., a0:g0], z64, m[..., g0:RW_COLS], zend], axis=-1)

    w = w_in[l]
    w_all = jnp.concatenate([rw_layout(w[:, :RW_COLS]), w[:, RW_COLS:]], axis=1).astype(BF16)
    row = lambda a: a.reshape(1, -1)
    return {
        "w_all": w_all,
        "mu": rw_layout(rw_mu[l]).reshape(1, 1, SEG),
        "w0": row(rw_w0[l][hm]),
        "w2": _pad_rows(rw_w2[l][:, hm], LANES).astype(BF16),
        "a0": row(rw_a0[l][hm]),
        "a2": _pad_rows(rw_a2[l][:, hm], LANES).astype(BF16),
        "g2": rw_g2[l][:, hm].astype(BF16),
        "k_k": row(rw_k_k[l][hm]),
        "k_a": row(rw_k_a[l][hm]),
        "r_k": row(rw_r_k[l].reshape(-1)[hm]),
        "ln_w": row(rw_ln_w[l][hm]),
        "ln_b": row(rw_ln_b[l][hm]),
        "w_up_a": w_up_a[l][hm, :].astype(BF16),
        "w_up_b": w_up_b[l].astype(BF16),
        "w_out": w_out[l].astype(BF16),
        "norm_ffn_g": row(norm_ffn_g[l]),
        "w_q": peer_w_q[l].astype(BF16),
        "keys": peer_keys[l].reshape(2 * PEER_HEADS, PEER_KEYS, PEER_QDIM // 2).astype(BF16),
    }


def _layer(l, x, shift0, wkv0, hg0, p, norm_mix_g, hg_lb_logits, hg_norm_g, u_tab, v_tab, out_norm_g):
    b, t, _ = x.shape
    n = b * t
    x2 = x.reshape(n, D_MODEL)
    gmix = norm_mix_g.reshape(1, D_MODEL)
    z2 = _proj(x2, gmix, p["w_all"], do_norm=True, nseg=3)
    new_shift = _rms_rows(x[:, -1], gmix)
    if shift0 is None:
        zfirst = jnp.zeros((b, 1, SEG), F32)
        h0 = jnp.zeros((b, HG_HEADS, HG_DK, HG_DV), F32)
    else:
        zfirst = _proj(shift0, gmix, p["w_all"], do_norm=False, nseg=1).reshape(b, 1, SEG)
        h0 = hg0
    nkk, w, bb, k2, r, v, g, bonus = _rwkv_pre(z2.reshape(b, t, 3 * SEG), zfirst, p)
    o_raw, new_wkv = _scan_unlayout(*_rwkv_scan(*_scan_layouts(nkk, w, bb, k2, r, v, wkv0)))
    o_b, new_hg = _hgrn(z2, hg_lb_logits, hg_norm_g.reshape(1, HG_DV), h0, b=b, t=t, layer=l)
    flat = lambda a: a.reshape(n, RW_WIDTH)
    x1, xn, st = _merge(x2, flat(o_raw), flat(bonus), flat(g), o_b, z2, p)
    a_idx, b_idx, gates = _topk(st)
    y = _peer(x1, xn, a_idx, b_idx, gates, u_tab, v_tab, out_norm_g.reshape(1, D_MODEL))
    return y.reshape(b, t, D_MODEL), new_shift, new_wkv, new_hg


def kernel(x_prompt, x_sample, state_rwkv_shift, state_rwkv_wkv, state_hgrn, norm_mix_g, w_in, rw_mu, rw_w0,
           rw_w2, rw_a0, rw_a2, rw_g2, rw_k_k, rw_k_a, rw_r_k, rw_ln_w, rw_ln_b, hg_lb_logits, hg_norm_g,
           w_up_a, w_up_b, w_out, norm_ffn_g, peer_w_q, peer_keys, peer_u, peer_v, norm_final_g):
    depth = w_in.shape[0]
    assert depth == 1, "the final rmsnorm is fused into the last layer's expert kernel"
    outs = []
    for l in range(depth):
        p = _layer_params(l, w_in, rw_mu, rw_w0, rw_w2, rw_a0, rw_a2, rw_g2, rw_k_k, rw_k_a, rw_r_k,
                          rw_ln_w, rw_ln_b, w_up_a, w_up_b, w_out, norm_ffn_g, peer_w_q, peer_keys)
        u_tab = peer_u[l].astype(BF16)
        v_tab = peer_v[l].astype(BF16)
        common = (p, norm_mix_g[l], hg_lb_logits, hg_norm_g[l], u_tab, v_tab, norm_final_g)
        yp, p_shift, p_wkv, p_hg = _layer(l, x_prompt, None, None, None, *common)
        ys, s_shift, s_wkv, s_hg = _layer(l, x_sample, state_rwkv_shift[l], state_rwkv_wkv[l],
                                          state_hgrn[l], *common)
        outs = (yp, ys, p_shift[None], p_wkv[None], p_hg[None], s_shift[None], s_wkv[None], s_hg[None])
    return outs
```

```python
import functools

import numpy as np
import jax
import jax.numpy as jnp
from jax import lax
from jax.experimental import pallas as pl
from jax.experimental.pallas import tpu as pltpu

F32 = jnp.float32
BF16 = jnp.bfloat16

D_MODEL = 1024
RW_HEADS = 8
RW_HEAD_DIM = 64
RW_WIDTH = 512
RW_DECAY_RANK = 64
RW_AAA_RANK = 64
RW_GATE_RANK = 128
RW_COLS = 1792
RW_GN_EPS = 64e-5
HG_HEADS = 4
HG_DK = 128
HG_DV = 128
HG_COLS = 2048
HG_CHUNK = 32
GATE_COLS = 2048
PEER_KEYS = 128
PEER_HEADS = 8
PEER_TOPK = 16
PEER_QDIM = 256
NORM_EPS = 1e-6

LANES = 128
SEG = 2048
RW_PAD = 1920
SCAN_NB = 8
SCAN_KH = RW_HEAD_DIM // (LANES // (RW_HEADS * SCAN_NB))
SCAN_ACCS = 4
SCAN_UNROLL = 4
HG_TILE_UNROLL = 16
PEER_CHUNK = 256
PEER_STEP = 1024
PEER_TOKENS = 512
VMEM_LIMIT = 60000 * 1024

_HM = (np.arange(RW_WIDTH) % RW_HEADS) * RW_HEAD_DIM + np.arange(RW_WIDTH) // RW_HEADS


def _cparams(sem):
    return pltpu.CompilerParams(dimension_semantics=sem, vmem_limit_bytes=VMEM_LIMIT)


def _sigmoid(x):
    return 1.0 / (1.0 + jnp.exp(-x))


def _rms(x, g):
    return x * lax.rsqrt(jnp.mean(x * x, axis=-1, keepdims=True) + NORM_EPS) * g


def _headsum(x):
    s = x[:, 0:128] + x[:, 128:256] + x[:, 256:384] + x[:, 384:512]
    for sh in (64, 32, 16, 8):
        s = s + pltpu.roll(s, sh, axis=1)
    return jnp.concatenate([s, s, s, s], axis=1)


def _dot(a, b):
    return jnp.dot(a, b, preferred_element_type=F32)


def _dot_nt(a, b):
    return lax.dot_general(a, b, (((1,), (1,)), ((), ())), preferred_element_type=F32)


def _proj_body(x_ref, g_ref, w_ref, o_ref, *, do_norm):
    x = x_ref[...]
    if do_norm:
        x = _rms(x, g_ref[...])
    xb = x.astype(BF16)
    for c in range(0, SEG, 512):
        o_ref[:, c:c + 512] = _dot(xb, w_ref[:, c:c + 512])


def _proj(x, g, w, *, do_norm, nseg):
    n = x.shape[0]
    tm = min(512, n)
    return pl.pallas_call(
        functools.partial(_proj_body, do_norm=do_norm),
        grid=(nseg, n // tm),
        in_specs=[pl.BlockSpec((tm, D_MODEL), lambda c, i: (i, 0)),
                  pl.BlockSpec((1, D_MODEL), lambda c, i: (0, 0)),
                  pl.BlockSpec((D_MODEL, SEG), lambda c, i: (0, c))],
        out_specs=pl.BlockSpec((tm, SEG), lambda c, i: (i, c)),
        out_shape=jax.ShapeDtypeStruct((n, nseg * SEG), F32),
        compiler_params=_cparams(("arbitrary", "arbitrary")),
        name="proj",
    )(x, g, w)


def _rms_rows_body(x_ref, g_ref, o_ref):
    o_ref[...] = _rms(x_ref[...], g_ref[...])


def _rms_rows(x, g):
    return pl.pallas_call(
        _rms_rows_body,
        out_shape=jax.ShapeDtypeStruct(x.shape, F32),
        name="rms_rows",
    )(x, g)


def _rwkv_pre_body(z_ref, zp_ref, f_ref, mu_ref, w0_ref, w2_ref, a0_ref, a2_ref, g2_ref,
                   kk_ref, ka_ref, rk_ref,
                   nkk_o, w_o, bb_o, k2_o, r_o, v_o, g_o, bon_o):
    j = pl.program_id(1)
    z = z_ref[...]
    bt, tt, c = z.shape
    pos = lax.broadcasted_iota(jnp.int32, z.shape, 1)
    first = jnp.where(j == 0, f_ref[...], zp_ref[:, 7:8, :])
    zprev = jnp.where(pos == 0, first, pltpu.roll(z, 1, axis=1))
    zs = (z + (zprev - z) * mu_ref[...]).reshape(bt * tt, c)
    r = zs[:, 0:512]
    k = zs[:, 512:1024]
    v = zs[:, 1024:1536]
    dl = zs[:, 1536:1664]
    al = zs[:, 1664:1792]
    gl = zs[:, 1792:1920]
    y = w0_ref[...] + _dot(jnp.tanh(dl).astype(BF16), w2_ref[...])
    softplus_neg = jnp.maximum(-y, 0.0) + jnp.log(1.0 + jnp.exp(-jnp.abs(y)))
    decay = jnp.exp(-jnp.exp(-softplus_neg - 0.5))
    a = _sigmoid(a0_ref[...] + _dot(al.astype(BF16), a2_ref[...]))
    g = _dot(_sigmoid(gl).astype(BF16), g2_ref[...])
    kk = k * kk_ref[...]
    kk = kk / jnp.maximum(jnp.sqrt(_headsum(kk * kk)), 1e-12)
    k2 = k * (1.0 + (a - 1.0) * ka_ref[...])
    bonus = _headsum(r * k2 * rk_ref[...]) * v
    for ref, val in ((nkk_o, -kk), (w_o, decay), (bb_o, kk * a), (k2_o, k2), (r_o, r), (v_o, v), (g_o, g),
                     (bon_o, bonus)):
        ref[...] = val.reshape(ref.shape)


def _rwkv_pre(z3, zfirst, p):
    b, t, _ = z3.shape
    if t >= 256:
        bt, tt = 1, 256
    else:
        bt, tt = min(b, 256 // t), t
    tpb = tt // 8
    row = lambda bi, j: (bi, j, 0)
    const2 = lambda bi, j: (0, 0)
    vec = pl.BlockSpec((1, RW_WIDTH), const2)
    low = pl.BlockSpec((LANES, RW_WIDTH), const2)
    out_spec = pl.BlockSpec((bt, tt, RW_WIDTH), row)
    out_shape = jax.ShapeDtypeStruct((b, t, RW_WIDTH), F32)
    return pl.pallas_call(
        _rwkv_pre_body,
        grid=(b // bt, t // tt),
        in_specs=[pl.BlockSpec((bt, tt, SEG), row),
                  pl.BlockSpec((bt, 8, SEG), lambda bi, j: (bi, jnp.maximum(j * tpb - 1, 0), 0)),
                  pl.BlockSpec((bt, 1, SEG), lambda bi, j: (bi, 0, 0)),
                  pl.BlockSpec((1, 1, SEG), lambda bi, j: (0, 0, 0)),
                  vec, low, vec, low, low, vec, vec, vec],
        out_specs=[out_spec] * 8,
        out_shape=[out_shape] * 8,
        compiler_params=_cparams(("arbitrary", "arbitrary")),
        name="rwkv_pre",
    )(z3, z3, zfirst, p["mu"], p["w0"], p["w2"], p["a0"], p["a2"], p["g2"],
      p["k_k"], p["k_a"], p["r_k"])


def _rwkv_scan_body(nkk_ref, w_ref, bb_ref, k2_ref, r_ref, vc_ref, s0_ref, o_ref, sf_ref, s_scr, *, tc):
    j = pl.program_id(1)

    @pl.when(j == 0)
    def _():
        s_scr[...] = s0_ref[0]

    def halfsum(parts):
        while len(parts) > 1:
            parts = [parts[i] + parts[i + 1] for i in range(0, len(parts), 2)]
        return parts[0] + pltpu.roll(parts[0], LANES // 2, axis=1)

    def step(t, carry):
        nkk = nkk_ref[0, t]
        wt = w_ref[0, t]
        bbt = bb_ref[0, t]
        k2t = k2_ref[0, t]
        rt = r_ref[0, t]
        vcol = vc_ref[0, t]
        accs = [None] * SCAN_ACCS
        for kh in range(SCAN_KH):
            p = s_scr[kh] * nkk[kh:kh + 1]
            accs[kh % SCAN_ACCS] = p if kh < SCAN_ACCS else accs[kh % SCAN_ACCS] + p
        sa = halfsum(accs)
        accs = [None] * SCAN_ACCS
        for kh in range(SCAN_KH):
            s = s_scr[kh] * wt[kh:kh + 1] + sa * bbt[kh:kh + 1] + vcol * k2t[kh:kh + 1]
            s_scr[kh] = s
            q = s * rt[kh:kh + 1]
            accs[kh % SCAN_ACCS] = q if kh < SCAN_ACCS else accs[kh % SCAN_ACCS] + q
        o_ref[0, t] = halfsum(accs)
        return carry

    lax.fori_loop(0, tc, step, 0, unroll=SCAN_UNROLL if tc % SCAN_UNROLL == 0 else 1)

    @pl.when(j == pl.num_programs(1) - 1)
    def _():
        sf_ref[0] = s_scr[...]


def _rwkv_scan(rows, vcols, s0):
    g, t = vcols.shape[:2]
    tc = min(64, t)
    row = pl.BlockSpec((1, tc, SCAN_KH, LANES), lambda gi, j: (gi, j, 0, 0))
    col = pl.BlockSpec((1, tc, RW_HEAD_DIM, LANES), lambda gi, j: (gi, j, 0, 0))
    st = pl.BlockSpec((1, SCAN_KH, RW_HEAD_DIM, LANES), lambda gi, j: (gi, 0, 0, 0))
    return pl.pallas_call(
        functools.partial(_rwkv_scan_body, tc=tc),
        grid=(g, t // tc),
        in_specs=[row] * 5 + [col, st],
        out_specs=[col, st],
        out_shape=[jax.ShapeDtypeStruct(vcols.shape, F32), jax.ShapeDtypeStruct(s0.shape, F32)],
        scratch_shapes=[pltpu.VMEM((SCAN_KH, RW_HEAD_DIM, LANES), F32)],
        compiler_params=_cparams(("arbitrary", "arbitrary")),
        name="rwkv_scan",
    )(*rows, vcols, s0)


PACK_STEPS = LANES // SCAN_NB


def _slot_rows(q, ts):
    return pl.ds((q // SCAN_NB) * ts * SCAN_NB + q % SCAN_NB, ts, stride=SCAN_NB)


def _rows_to_step_lanes(x_ref, m_s, mt_s):
    nq, ts, _ = x_ref.shape
    for j in range(RW_WIDTH // LANES):
        for q in range(nq):
            m_s[j, _slot_rows(q, ts), :] = x_ref[q, :, j * LANES:(j + 1) * LANES]
        mt_s[j * LANES:(j + 1) * LANES, :] = m_s[j].T


def _step_lanes_to_rows(mt_s, m_s, o_ref):
    nq, ts, _ = o_ref.shape
    for j in range(RW_WIDTH // LANES):
        m_s[j] = mt_s[j * LANES:(j + 1) * LANES, :].T
        for q in range(nq):
            o_ref[q, :, j * LANES:(j + 1) * LANES] = m_s[j, _slot_rows(q, ts), :]


def _rot_groups(x, groups):
    groups %= PACK_STEPS
    return pltpu.roll(x, groups * SCAN_NB, axis=1) if groups else x


def _gather_rotated(tiles):
    n = PACK_STEPS
    group = lax.broadcasted_iota(jnp.int32, (1, LANES), 1) // SCAN_NB
    lst = [tiles[(-m) % n] for m in range(n)]
    for k in range(n.bit_length() - 1):
        bit = ((group >> k) & 1) == 1
        lst = [jnp.where(bit, lst[(s - (1 << k)) % n], lst[s]) for s in range(n)]
    return lst


def _pack_one(x_ref, y_ref, m_s, mt_s, nblk, dup):
    nrow = RW_WIDTH // nblk
    ts = y_ref.shape[1]
    _rows_to_step_lanes(x_ref, m_s, mt_s)
    zero = jnp.zeros((nrow, LANES), F32)
    tiles = [_rot_groups(mt_s[pl.ds(blk, nrow, stride=nblk), :], blk) if blk < nblk else zero
             for blk in range(PACK_STEPS)]
    for s, acc in enumerate(_gather_rotated(tiles)):
        y = _rot_groups(acc, -s)
        y_ref[s // ts, s % ts] = y + pltpu.roll(y, LANES // 2, axis=1) if dup else y


def _scan_pack_body(nkk_ref, w_ref, bb_ref, k2_ref, r_ref, v_ref,
                    nkk_o, w_o, bb_o, k2_o, r_o, vc_o, m_s, mt_s):
    pairs = ((nkk_ref, nkk_o), (w_ref, w_o), (bb_ref, bb_o), (k2_ref, k2_o), (r_ref, r_o))
    for i, (x_ref, y_ref) in enumerate(pairs):
        _pack_one(x_ref, y_ref, m_s.at[i], mt_s.at[i], 2 * RW_HEADS, False)
    _pack_one(v_ref, vc_o, m_s.at[5], mt_s.at[5], RW_HEADS, True)


def _scan_unpack_body(o_ref, out_ref, m_s, mt_s):
    m_s, mt_s = m_s.at[0], mt_s.at[0]
    ts = o_ref.shape[1]
    lane = lax.broadcasted_iota(jnp.int32, (1, LANES), 1)
    tiles = [_rot_groups(jnp.where(lane < LANES // 2, o_ref[s // ts, s % ts], 0.0), s) for s in range(PACK_STEPS)]
    for h, acc in enumerate(_gather_rotated(tiles)[:RW_HEADS]):
        mt_s[pl.ds(h, RW_HEAD_DIM, stride=RW_HEADS), :] = _rot_groups(acc, -h)
    _step_lanes_to_rows(mt_s, m_s, out_ref)


def _pack_scratch(n):
    return [pltpu.VMEM((n, RW_WIDTH // LANES, LANES, LANES), F32), pltpu.VMEM((n, RW_WIDTH, LANES), F32)]


def _pack_blocking(g, t):
    ts = min(t, PACK_STEPS)
    gp = PACK_STEPS // ts
    return (ts, gp) if t % ts == 0 and PACK_STEPS % ts == 0 and g % gp == 0 else None


def _scan_pack(arrs, ts, gp):
    b, t, _ = arrs[0].shape
    g = b // SCAN_NB
    rows = pl.BlockSpec((gp * SCAN_NB, ts, RW_WIDTH), lambda gi, j: (gi, j, 0))
    kspec = pl.BlockSpec((gp, ts, SCAN_KH, LANES), lambda gi, j: (gi, j, 0, 0))
    vspec = pl.BlockSpec((gp, ts, RW_HEAD_DIM, LANES), lambda gi, j: (gi, j, 0, 0))
    return pl.pallas_call(
        _scan_pack_body,
        grid=(g // gp, t // ts),
        in_specs=[rows] * 6,
        out_specs=[kspec] * 5 + [vspec],
        out_shape=[jax.ShapeDtypeStruct((g, t, SCAN_KH, LANES), F32)] * 5
        + [jax.ShapeDtypeStruct((g, t, RW_HEAD_DIM, LANES), F32)],
        scratch_shapes=_pack_scratch(6),
        compiler_params=_cparams(("arbitrary", "arbitrary")),
        name="scan_pack",
    )(*arrs)


def _scan_unpack(o, ts, gp):
    g, t = o.shape[:2]
    return pl.pallas_call(
        _scan_unpack_body,
        grid=(g // gp, t // ts),
        in_specs=[pl.BlockSpec((gp, ts, RW_HEAD_DIM, LANES), lambda gi, j: (gi, j, 0, 0))],
        out_specs=pl.BlockSpec((gp * SCAN_NB, ts, RW_WIDTH), lambda gi, j: (gi, j, 0)),
        out_shape=jax.ShapeDtypeStruct((g * SCAN_NB, t, RW_WIDTH), F32),
        scratch_shapes=_pack_scratch(1),
        compiler_params=_cparams(("arbitrary", "arbitrary")),
        name="scan_unpack",
    )(o)


def _seq_to_lanes(x):
    b, t, c = x.shape
    return x.reshape(b // SCAN_NB, SCAN_NB, t, c).transpose(0, 2, 3, 1)


def _scan_layouts(nkk, w, bb, k2, r, v, wkv0):
    b, t, _ = v.shape
    g = b // SCAN_NB
    blocking = _pack_blocking(g, t)
    if blocking:
        *rows, vc = _scan_pack([nkk, w, bb, k2, r, v], *blocking)
    else:
        rows = [_seq_to_lanes(a).reshape(g, t, SCAN_KH, LANES) for a in (nkk, w, bb, k2, r)]
        vc = _seq_to_lanes(v).reshape(g, t, RW_HEAD_DIM, LANES // 2)
        vc = jnp.concatenate([vc, vc], axis=-1)
    if wkv0 is None:
        s0 = jnp.zeros((g, SCAN_KH, RW_HEAD_DIM, LANES), F32)
    else:
        s0 = wkv0.reshape(g, SCAN_NB, RW_HEADS, RW_HEAD_DIM, SCAN_KH, 2)
        s0 = s0.transpose(0, 4, 3, 5, 2, 1).reshape(g, SCAN_KH, RW_HEAD_DIM, LANES)
    return rows, vc, s0


def _scan_unlayout(o, sf):
    g, t = o.shape[:2]
    blocking = _pack_blocking(g, t)
    if blocking:
        o = _scan_unpack(o, *blocking)
    else:
        o = o[..., :LANES // 2].reshape(g, t, RW_WIDTH, SCAN_NB)
        o = o.transpose(0, 3, 1, 2).reshape(g * SCAN_NB, t, RW_WIDTH)
    sf = sf.reshape(g, SCAN_KH, RW_HEAD_DIM, 2, RW_HEADS, SCAN_NB)
    sf = sf.transpose(0, 5, 4, 2, 1, 3).reshape(g * SCAN_NB, RW_HEADS, RW_HEAD_DIM, RW_HEAD_DIM)
    return o, sf


def _hgrn_body(q_ref, f_ref, i_ref, g_ref, lbl_ref, ng_ref, st_ref, o_ref, so_ref,
               qin_s, kin_s, kout_s, d_s, *, chunk, per_seq, layer):
    rows = q_ref.shape[0]
    lg = lbl_ref[...]
    e = jnp.exp(lg - jnp.max(lg, axis=0, keepdims=True))
    sm = e / jnp.sum(e, axis=0, keepdims=True)
    lb = jnp.sum(sm[0:layer + 1], axis=0, keepdims=True)
    f = lb + (1.0 - lb) * _sigmoid(f_ref[...])
    logf = jnp.log(f)
    kf = 1.0 - f
    pos = lax.broadcasted_iota(jnp.int32, (rows, LANES), 0) % chunk
    cum = logf
    suf = logf
    s = 1
    while s < chunk:
        cum = cum + jnp.where(pos >= s, pltpu.roll(cum, s, axis=0), 0.0)
        suf = suf + jnp.where(pos < chunk - s, pltpu.roll(suf, rows - s, axis=0), 0.0)
        s *= 2
    qin_s[...] = q_ref[...] * jnp.exp(cum)
    kin_s[...] = kf * jnp.exp(-cum)
    kout_s[...] = kf * jnp.exp(suf - logf)
    d_s[...] = jnp.exp(cum + suf - logf)
    so_ref[...] = st_ref[...]

    nch = LANES // chunk
    ri = lax.broadcasted_iota(jnp.int32, (LANES, LANES), 0)
    ci = lax.broadcasted_iota(jnp.int32, (LANES, LANES), 1)
    causal = (ri // chunk == ci // chunk) & (ci <= ri)
    ng = ng_ref[...]

    def tile(ti, carry):
        r0 = pl.multiple_of(ti * LANES, LANES)
        qi = qin_s[pl.ds(r0, LANES), :]
        ki = kin_s[pl.ds(r0, LANES), :]
        ko = kout_s[pl.ds(r0, LANES), :]
        dd = d_s[pl.ds(r0, LANES), :]
        vv = i_ref[pl.ds(r0, LANES), :]
        qb = qi.astype(BF16)
        vb = vv.astype(BF16)
        att = jnp.where(causal, _dot_nt(qb, ki.astype(BF16)), 0.0)
        o = _dot(att.astype(BF16), vb)
        kot = ko.T
        if per_seq:
            sall = jnp.concatenate([so_ref[c] for c in range(nch)], axis=1).astype(BF16)
            big = _dot(qb, sall)
            inter = jnp.zeros((LANES, LANES), F32)
            for c in range(nch):
                inter = inter + jnp.where(ri // chunk == c, big[:, c * LANES:(c + 1) * LANES], 0.0)
            o = o + inter
        dcols = [jnp.broadcast_to(dd[c * chunk:c * chunk + 1, :], (LANES, LANES)).T for c in range(nch)]
        kvs = [_dot(jnp.where(ci // chunk == c, kot, 0.0).astype(BF16), vb) for c in range(nch)]
        if per_seq:
            for c in range(nch):
                so_ref[c] = so_ref[c] * dcols[c] + kvs[c]
        else:
            st = so_ref[0]
            parts = []
            for c in range(nch):
                parts.append(_dot(qb[c * chunk:(c + 1) * chunk], st.astype(BF16)))
                st = st * dcols[c] + kvs[c]
            so_ref[0] = st
            o = o + jnp.concatenate(parts, axis=0)
        o = o * lax.rsqrt(jnp.mean(o * o, axis=-1, keepdims=True) + NORM_EPS) * ng
        gate = g_ref[pl.ds(r0, LANES), :]
        o_ref[pl.ds(r0, LANES), :] = o * (gate * _sigmoid(gate))
        return carry

    ntile = rows // LANES
    lax.fori_loop(0, ntile, tile, 0, unroll=HG_TILE_UNROLL if ntile % HG_TILE_UNROLL == 0 else 1)


def _hgrn(z2, lb_logits, norm_g, st0, *, b, t, layer):
    chunk = min(HG_CHUNK, t)
    per_seq = t < LANES
    if per_seq:
        rows = LANES
        ns = LANES // t
    else:
        rows = t
        ns = 1
    nblk = (b * t) // rows
    c0 = SEG // LANES

    def col(off):
        return pl.BlockSpec((rows, LANES), lambda g, h: (g, c0 + off * HG_HEADS + h))

    st_spec = pl.BlockSpec((ns, None, HG_DK, HG_DV), lambda g, h: (g, h, 0, 0))
    nd = lb_logits.shape[0]
    scr = pltpu.VMEM((rows, LANES), F32)
    return pl.pallas_call(
        functools.partial(_hgrn_body, chunk=chunk, per_seq=per_seq, layer=layer),
        grid=(nblk, HG_HEADS),
        in_specs=[col(0), col(1), col(2), col(3),
                  pl.BlockSpec((nd, LANES), lambda g, h: (0, h)),
                  pl.BlockSpec((1, LANES), lambda g, h: (0, 0)),
                  st_spec],
        out_specs=[pl.BlockSpec((rows, LANES), lambda g, h: (g, h)), st_spec],
        out_shape=[jax.ShapeDtypeStruct((b * t, HG_HEADS * HG_DV), F32),
                   jax.ShapeDtypeStruct(st0.shape, F32)],
        scratch_shapes=[scr, scr, scr, scr],
        compiler_params=_cparams(("arbitrary", "arbitrary")),
        name="hgrn",
    )(z2, z2, z2, z2, lb_logits, norm_g, st0)


def _merge_body(x_ref, o_ref, bon_ref, g_ref, ob_ref, gate_ref, lnw_ref, lnb_ref,
                wa_ref, wb_ref, wo_ref, nf_ref, wq_ref, keys_ref,
                x1_o, xn_o, st_o):
    o = o_ref[...]
    d = o - _headsum(o) * (1.0 / RW_HEAD_DIM)
    var = _headsum(d * d) * (1.0 / RW_HEAD_DIM)
    oa = d * lax.rsqrt(var + RW_GN_EPS) * lnw_ref[...] + lnb_ref[...]
    oa = (oa + bon_ref[...]) * g_ref[...]
    ya = _dot(oa.astype(BF16), wa_ref[...])
    yb = _dot(ob_ref[...].astype(BF16), wb_ref[...])
    gate = gate_ref[...]
    merged = _sigmoid(gate[:, 0:D_MODEL]) * ya + _sigmoid(gate[:, D_MODEL:]) * yb
    x1 = x_ref[...] + _dot(merged.astype(BF16), wo_ref[...])
    x1_o[...] = x1
    xn = _rms(x1, nf_ref[...]).astype(BF16)
    xn_o[...] = xn
    for c in range(0, PEER_HEADS * PEER_QDIM, 512):
        q = _dot(xn, wq_ref[:, c:c + 512]).astype(BF16)
        for s in range(4):
            hc = c // LANES + s
            st_o[hc] = _dot_nt(keys_ref[hc], q[:, s * LANES:(s + 1) * LANES])


def _merge(x, o_raw, bonus, g, o_b, z2, p):
    n = x.shape[0]
    tm = min(256, n)
    nhc = 2 * PEER_HEADS
    r512 = pl.BlockSpec((tm, RW_WIDTH), lambda i: (i, 0))
    r1024 = pl.BlockSpec((tm, D_MODEL), lambda i: (i, 0))

    def full(a):
        nd = a.ndim
        return pl.BlockSpec(a.shape, lambda i: (0,) * nd)

    ws = [p["ln_w"], p["ln_b"], p["w_up_a"], p["w_up_b"], p["w_out"], p["norm_ffn_g"], p["w_q"], p["keys"]]
    return pl.pallas_call(
        _merge_body,
        grid=(n // tm,),
        in_specs=[r1024, r512, r512, r512, r512,
                  pl.BlockSpec((tm, SEG), lambda i: (i, 2))] + [full(a) for a in ws],
        out_specs=[r1024, r1024, pl.BlockSpec((nhc, PEER_KEYS, tm), lambda i: (0, 0, i))],
        out_shape=[jax.ShapeDtypeStruct((n, D_MODEL), F32),
                   jax.ShapeDtypeStruct((n, D_MODEL), BF16),
                   jax.ShapeDtypeStruct((nhc, PEER_KEYS, n), F32)],
        compiler_params=_cparams(("arbitrary",)),
        name="merge",
    )(x, o_raw, bonus, g, o_b, z2, *ws)


def _top16(xs, payloads):
    nrow = xs[0].shape[0]
    rowi = lax.broadcasted_iota(jnp.int32, xs[0].shape, 0).astype(F32)
    slot = lax.broadcasted_iota(jnp.int32, (PEER_TOPK, LANES), 0)
    zero = jnp.zeros((PEER_TOPK, LANES), F32)

    sub = 8
    rowc = [rowi[i:i + sub] for i in range(0, nrow, sub)]

    def argmax_rows(x):
        pairs = [(x[i:i + sub], rc) for i, rc in zip(range(0, nrow, sub), rowc)]
        while len(pairs) > 1:
            nxt = []
            for j in range(0, len(pairs) - 1, 2):
                (av, ai), (bv, bi) = pairs[j], pairs[j + 1]
                keep = av >= bv
                nxt.append((jnp.where(keep, av, bv), jnp.where(keep, ai, bi)))
            if len(pairs) % 2:
                nxt.append(pairs[-1])
            pairs = nxt
        v, i = pairs[0]
        m = jnp.max(v, axis=0, keepdims=True)
        return m, jnp.min(jnp.where(v == m, i, float(nrow)), axis=0, keepdims=True)

    def it(r, carry):
        here = slot == r
        new = []
        for (x, vals, outs), pay in zip(carry, payloads):
            m, idx = argmax_rows(x)
            vals = jnp.where(here, m, vals)
            picked = (idx,) if pay is None else pay(idx)
            outs = tuple(jnp.where(here, p, o) for p, o in zip(picked, outs))
            new.append((jnp.where(rowi == idx, -jnp.inf, x), vals, outs))
        return tuple(new)

    init = tuple((x, zero, (zero,) if pay is None else (zero, zero)) for x, pay in zip(xs, payloads))
    return [(vals, outs) for _, vals, outs in lax.fori_loop(0, PEER_TOPK, it, init)]


def _pair_candidates(v1, i1, v2, i2):
    sub = 8
    row8 = lax.broadcasted_iota(jnp.int32, (sub, LANES), 0)
    cand = [v1[0:1] + v2]
    for k1 in range(1, sub):
        cand.append(jnp.where(row8 < PEER_TOPK // (k1 + 1), v1[k1:k1 + 1] + v2[0:sub], -jnp.inf))
    cand.append(v1[sub:] + v2[0:1])
    tail = PEER_TOPK + sub * (sub - 1)
    slot = lax.broadcasted_iota(jnp.int32, (PEER_TOPK, LANES), 0)

    def keys_of(idx):
        r = idx.astype(jnp.int32)
        mid = r - PEER_TOPK
        k1 = jnp.where(r < PEER_TOPK, 0, jnp.where(r < tail, 1 + (mid >> 3), r - tail + sub))
        k2 = jnp.where(r < PEER_TOPK, r, jnp.where(r < tail, mid & (sub - 1), 0))
        return (jnp.sum(jnp.where(slot == k1, i1, 0.0), axis=0, keepdims=True),
                jnp.sum(jnp.where(slot == k2, i2, 0.0), axis=0, keepdims=True))

    return jnp.concatenate(cand, axis=0), keys_of


TOPK_HEADS_PER_ROUND = 2


def _topk_body(st_ref, a_o, b_o, g_o, a_s, b_s, g_s):
    nh = TOPK_HEADS_PER_ROUND

    def group(gi, carry):
        halves = _top16([st_ref[2 * nh * gi + i] for i in range(2 * nh)], [None] * (2 * nh))
        cands = [_pair_candidates(halves[2 * i][0], halves[2 * i][1][0], halves[2 * i + 1][0], halves[2 * i + 1][1][0])
                 for i in range(nh)]
        picks = _top16([c[0] for c in cands], [c[1] for c in cands])
        for i, (cv, (ea, eb)) in enumerate(picks):
            ex = jnp.exp(cv - cv[0:1])
            r0 = pl.multiple_of((gi * nh + i) * PEER_TOPK, PEER_TOPK)
            a_s[pl.ds(r0, PEER_TOPK), :] = ea
            b_s[pl.ds(r0, PEER_TOPK), :] = eb
            g_s[pl.ds(r0, PEER_TOPK), :] = ex / jnp.sum(ex, axis=0, keepdims=True)
        return carry

    lax.fori_loop(0, PEER_HEADS // nh, group, 0)
    a_o[...] = a_s[...].T
    b_o[...] = b_s[...].T
    g_o[...] = g_s[...].T


def _topk(st):
    nhc, nk, n = st.shape
    out = jax.ShapeDtypeStruct((n, LANES), F32)
    ospec = pl.BlockSpec((LANES, LANES), lambda i: (i, 0))
    scr = pltpu.VMEM((LANES, LANES), F32)
    return pl.pallas_call(
        _topk_body,
        grid=(n // LANES,),
        in_specs=[pl.BlockSpec((nhc, nk, LANES), lambda i: (0, 0, i))],
        out_specs=[ospec] * 3,
        out_shape=[out] * 3,
        scratch_shapes=[scr] * 3,
        compiler_params=_cparams(("arbitrary",)),
        name="topk",
    )(st)


def _peer_body(x1_ref, xn_ref, a_ref, b_ref, gt_ref, u_ref, v_ref, nf_ref, y_ref, gate_s, acc_s, *, tb):
    c = pl.program_id(1)

    @pl.when(c == 0)
    def _():
        acc_s[...] = jnp.zeros_like(acc_s)
        sub = lax.broadcasted_iota(jnp.int32, (PEER_KEYS, LANES), 0).astype(F32)

        def tok(t, carry):
            ar = a_ref[pl.ds(t, 1), :]
            br = b_ref[pl.ds(t, 1), :]
            gr = gt_ref[pl.ds(t, 1), :]
            oa = jnp.where(sub == ar, 1.0, 0.0).astype(BF16)
            ob = jnp.where(sub == br, gr, 0.0).astype(BF16)
            gate_s[pl.ds(pl.multiple_of(t * PEER_KEYS, PEER_KEYS), PEER_KEYS), :] = _dot_nt(oa, ob)
            return carry

        lax.fori_loop(0, tb, tok, 0, unroll=64)

    xn = xn_ref[...]
    parts = []
    for k in range(PEER_STEP // PEER_CHUNK):
        u = _dot_nt(xn, u_ref[k * PEER_CHUNK:(k + 1) * PEER_CHUNK, :])
        act = 0.5 * u * (1.0 + lax.erf(u * (2.0 ** -0.5)))
        row = c * (PEER_STEP // PEER_KEYS) + 2 * k
        g0 = gate_s[pl.ds(row, tb, stride=PEER_KEYS), :]
        g1 = gate_s[pl.ds(row + 1, tb, stride=PEER_KEYS), :]
        parts.append((jnp.concatenate([g0, g1], axis=1) * act).astype(BF16))
    acc_s[...] += _dot(jnp.concatenate(parts, axis=1), v_ref[...])

    @pl.when(c == pl.num_programs(1) - 1)
    def _():
        y_ref[...] = _rms(x1_ref[...] + acc_s[...], nf_ref[...])


def _peer(x1, xn, a_idx, b_idx, gates, u_tab, v_tab, norm_g):
    n = x1.shape[0]
    tb = min(PEER_TOKENS, n)
    ne = u_tab.shape[0]
    row = lambda i, c: (i, 0)
    return pl.pallas_call(
        functools.partial(_peer_body, tb=tb),
        grid=(n // tb, ne // PEER_STEP),
        in_specs=[pl.BlockSpec((tb, D_MODEL), row),
                  pl.BlockSpec((tb, D_MODEL), row),
                  pl.BlockSpec((tb, LANES), row),
                  pl.BlockSpec((tb, LANES), row),
                  pl.BlockSpec((tb, LANES), row),
                  pl.BlockSpec((PEER_STEP, D_MODEL), lambda i, c: (c, 0)),
                  pl.BlockSpec((PEER_STEP, D_MODEL), lambda i, c: (c, 0)),
                  pl.BlockSpec((1, D_MODEL), lambda i, c: (0, 0))],
        out_specs=pl.BlockSpec((tb, D_MODEL), row),
        out_shape=jax.ShapeDtypeStruct((n, D_MODEL), F32),
        scratch_shapes=[pltpu.VMEM((tb * PEER_KEYS, LANES), F32),
                        pltpu.VMEM((tb, D_MODEL), F32)],
        compiler_params=_cparams(("arbitrary", "arbitrary")),
        name="peer",
    )(x1, xn, a_idx, b_idx, gates, u_tab, v_tab, norm_g)


def _pad_rows(a, n):
    return jnp.concatenate([a, jnp.zeros((n - a.shape[0],) + a.shape[1:], a.dtype)], axis=0)


def _layer_params(l, w_in, rw_mu, rw_w0, rw_w2, rw_a0, rw_a2, rw_g2, rw_k_k, rw_k_a, rw_r_k, rw_ln_w,
                  rw_ln_b, w_up_a, w_up_b, w_out, norm_ffn_g, peer_w_q, peer_keys):
    hm = jnp.asarray(_HM)
    r0, k0, v0 = 0, RW_WIDTH, 2 * RW_WIDTH
    d0 = 3 * RW_WIDTH
    a0 = d0 + RW_DECAY_RANK
    g0 = a0 + RW_AAA_RANK

    def rw_layout(m):
        z64 = jnp.zeros(m.shape[:-1] + (64,), m.dtype)
        zend = jnp.zeros(m.shape[:-1] + (SEG - RW_PAD,), m.dtype)
        return jnp.concatenate([m[..., r0:k0][..., hm], m[..., k0:v0][..., hm], m[..., v0:d0][..., hm],
                                m[..., d0:a0], z64, m[..., a0:g0], z64, m[..., g0:RW_COLS], zend], axis=-1)

    w = w_in[l]
    w_all = jnp.concatenate([rw_layout(w[:, :RW_COLS]), w[:, RW_COLS:]], axis=1).astype(BF16)
    row = lambda a: a.reshape(1, -1)
    return {
        "w_all": w_all,
        "mu": rw_layout(rw_mu[l]).reshape(1, 1, SEG),
        "w0": row(rw_w0[l][hm]),
        "w2": _pad_rows(rw_w2[l][:, hm], LANES).astype(BF16),
        "a0": row(rw_a0[l][hm]),
        "a2": _pad_rows(rw_a2[l][:, hm], LANES).astype(BF16),
        "g2": rw_g2[l][:, hm].astype(BF16),
        "k_k": row(rw_k_k[l][hm]),
        "k_a": row(rw_k_a[l][hm]),
        "r_k": row(rw_r_k[l].reshape(-1)[hm]),
        "ln_w": row(rw_ln_w[l][hm]),
        "ln_b": row(rw_ln_b[l][hm]),
        "w_up_a": w_up_a[l][hm, :].astype(BF16),
        "w_up_b": w_up_b[l].astype(BF16),
        "w_out": w_out[l].astype(BF16),
        "norm_ffn_g": row(norm_ffn_g[l]),
        "w_q": peer_w_q[l].astype(BF16),
        "keys": peer_keys[l].reshape(2 * PEER_HEADS, PEER_KEYS, PEER_QDIM // 2).astype(BF16),
    }


def _layer(l, x, shift0, wkv0, hg0, p, norm_mix_g, hg_lb_logits, hg_norm_g, u_tab, v_tab, out_norm_g):
    b, t, _ = x.shape
    n = b * t
    x2 = x.reshape(n, D_MODEL)
    gmix = norm_mix_g.reshape(1, D_MODEL)
    z2 = _proj(x2, gmix, p["w_all"], do_norm=True, nseg=3)
    new_shift = _rms_rows(x[:, -1], gmix)
    if shift0 is None:
        zfirst = jnp.zeros((b, 1, SEG), F32)
        h0 = jnp.zeros((b, HG_HEADS, HG_DK, HG_DV), F32)
    else:
        zfirst = _proj(shift0, gmix, p["w_all"], do_norm=False, nseg=1).reshape(b, 1, SEG)
        h0 = hg0
    nkk, w, bb, k2, r, v, g, bonus = _rwkv_pre(z2.reshape(b, t, 3 * SEG), zfirst, p)
    o_raw, new_wkv = _scan_unlayout(*_rwkv_scan(*_scan_layouts(nkk, w, bb, k2, r, v, wkv0)))
    o_b, new_hg = _hgrn(z2, hg_lb_logits, hg_norm_g.reshape(1, HG_DV), h0, b=b, t=t, layer=l)
    flat = lambda a: a.reshape(n, RW_WIDTH)
    x1, xn, st = _merge(x2, flat(o_raw), flat(bonus), flat(g), o_b, z2, p)
    a_idx, b_idx, gates = _topk(st)
    y = _peer(x1, xn, a_idx, b_idx, gates, u_tab, v_tab, out_norm_g.reshape(1, D_MODEL))
    return y.reshape(b, t, D_MODEL), new_shift, new_wkv, new_hg


def kernel(x_prompt, x_sample, state_rwkv_shift, state_rwkv_wkv, state_hgrn, norm_mix_g, w_in, rw_mu, rw_w0,
           rw_w2, rw_a0, rw_a2, rw_g2, rw_k_k, rw_k_a, rw_r_k, rw_ln_w, rw_ln_b, hg_lb_logits, hg_norm_g,
           w_up_a, w_up_b, w_out, norm_ffn_g, peer_w_q, peer_keys, peer_u, peer_v, norm_final_g):
    depth = w_in.shape[0]
    assert depth == 1, "the final rmsnorm is fused into the last layer's expert kernel"
    outs = []
    for l in range(depth):
        p = _layer_params(l, w_in, rw_mu, rw_w0, rw_w2, rw_a0, rw_a2, rw_g2, rw_k_k, rw_k_a, rw_r_k,
                          rw_ln_w, rw_ln_b, w_up_a, w_up_b, w_out, norm_ffn_g, peer_w_q, peer_keys)
        u_tab = peer_u[l].astype(BF16)
        v_tab = peer_v[l].astype(BF16)
        common = (p, norm_mix_g[l], hg_lb_logits, hg_norm_g[l], u_tab, v_tab, norm_final_g)
        yp, p_shift, p_wkv, p_hg = _layer(l, x_prompt, None, None, None, *common)
        ys, s_shift, s_wkv, s_hg = _layer(l, x_sample, state_rwkv_shift[l], state_rwkv_wkv[l],
                                          state_hgrn[l], *common)
        outs = (yp, ys, p_shift[None], p_wkv[None], p_hg[None], s_shift[None], s_wkv[None], s_hg[None])
    return outs
```

```python
import functools

import numpy as np
import jax
import jax.numpy as jnp
from jax import lax
from jax.experimental import pallas as pl
from jax.experimental.pallas import tpu as pltpu

F32 = jnp.float32
BF16 = jnp.bfloat16

D_MODEL = 1024
RW_HEADS = 8
RW_HEAD_DIM = 64
RW_WIDTH = 512
RW_DECAY_RANK = 64
RW_AAA_RANK = 64
RW_GATE_RANK = 128
RW_COLS = 1792
RW_GN_EPS = 64e-5
HG_HEADS = 4
HG_DK = 128
HG_DV = 128
HG_COLS = 2048
HG_CHUNK = 32
GATE_COLS = 2048
PEER_KEYS = 128
PEER_HEADS = 8
PEER_TOPK = 16
PEER_QDIM = 256
NORM_EPS = 1e-6

LANES = 128
SEG = 2048
RW_PAD = 1920
SCAN_NB = 8
SCAN_KH = RW_HEAD_DIM // (LANES // (RW_HEADS * SCAN_NB))
SCAN_ACCS = 4
MERGE_TOKENS = 512
SCAN_UNROLL = 8
HG_TILE_UNROLL = 16
PEER_CHUNK = 256
PEER_STEP = 1024
PEER_TOKENS = 512
VMEM_LIMIT = 60000 * 1024

_HM = (np.arange(RW_WIDTH) % RW_HEADS) * RW_HEAD_DIM + np.arange(RW_WIDTH) // RW_HEADS


def _cparams(sem):
    return pltpu.CompilerParams(dimension_semantics=sem, vmem_limit_bytes=VMEM_LIMIT)


def _sigmoid(x):
    return 1.0 / (1.0 + jnp.exp(-x))


def _rms(x, g):
    return x * lax.rsqrt(jnp.mean(x * x, axis=-1, keepdims=True) + NORM_EPS) * g


def _headsum(x):
    s = x[:, 0:128] + x[:, 128:256] + x[:, 256:384] + x[:, 384:512]
    for sh in (64, 32, 16, 8):
        s = s + pltpu.roll(s, sh, axis=1)
    return jnp.concatenate([s, s, s, s], axis=1)


def _dot(a, b):
    return jnp.dot(a, b, preferred_element_type=F32)


def _dot_nt(a, b):
    return lax.dot_general(a, b, (((1,), (1,)), ((), ())), preferred_element_type=F32)


def _proj_body(x_ref, g_ref, w_ref, o_ref, *, do_norm):
    x = x_ref[...]
    if do_norm:
        x = _rms(x, g_ref[...])
    xb = x.astype(BF16)
    for c in range(0, SEG, 512):
        o_ref[:, c:c + 512] = _dot(xb, w_ref[:, c:c + 512])


def _proj(x, g, w, *, do_norm, nseg):
    n = x.shape[0]
    tm = min(512, n)
    return pl.pallas_call(
        functools.partial(_proj_body, do_norm=do_norm),
        grid=(nseg, n // tm),
        in_specs=[pl.BlockSpec((tm, D_MODEL), lambda c, i: (i, 0)),
                  pl.BlockSpec((1, D_MODEL), lambda c, i: (0, 0)),
                  pl.BlockSpec((D_MODEL, SEG), lambda c, i: (0, c))],
        out_specs=pl.BlockSpec((tm, SEG), lambda c, i: (i, c)),
        out_shape=jax.ShapeDtypeStruct((n, nseg * SEG), F32),
        compiler_params=_cparams(("arbitrary", "arbitrary")),
        name="proj",
    )(x, g, w)


def _rms_rows_body(x_ref, g_ref, o_ref):
    o_ref[...] = _rms(x_ref[...], g_ref[...])


def _rms_rows(x, g):
    return pl.pallas_call(
        _rms_rows_body,
        out_shape=jax.ShapeDtypeStruct(x.shape, F32),
        name="rms_rows",
    )(x, g)


def _rwkv_pre_body(z_ref, zp_ref, f_ref, mu_ref, w0_ref, w2_ref, a0_ref, a2_ref, g2_ref,
                   kk_ref, ka_ref, rk_ref,
                   nkk_o, w_o, bb_o, k2_o, r_o, v_o, g_o, bon_o):
    j = pl.program_id(1)
    z = z_ref[...]
    bt, tt, c = z.shape
    pos = lax.broadcasted_iota(jnp.int32, z.shape, 1)
    first = jnp.where(j == 0, f_ref[...], zp_ref[:, 7:8, :])
    zprev = jnp.where(pos == 0, first, pltpu.roll(z, 1, axis=1))
    zs = (z + (zprev - z) * mu_ref[...]).reshape(bt * tt, c)
    r = zs[:, 0:512]
    k = zs[:, 512:1024]
    v = zs[:, 1024:1536]
    dl = zs[:, 1536:1664]
    al = zs[:, 1664:1792]
    gl = zs[:, 1792:1920]
    y = w0_ref[...] + _dot(jnp.tanh(dl).astype(BF16), w2_ref[...])
    softplus_neg = jnp.maximum(-y, 0.0) + jnp.log(1.0 + jnp.exp(-jnp.abs(y)))
    decay = jnp.exp(-jnp.exp(-softplus_neg - 0.5))
    a = _sigmoid(a0_ref[...] + _dot(al.astype(BF16), a2_ref[...]))
    g = _dot(_sigmoid(gl).astype(BF16), g2_ref[...])
    kk = k * kk_ref[...]
    kk = kk / jnp.maximum(jnp.sqrt(_headsum(kk * kk)), 1e-12)
    k2 = k * (1.0 + (a - 1.0) * ka_ref[...])
    bonus = _headsum(r * k2 * rk_ref[...]) * v
    for ref, val in ((nkk_o, -kk), (w_o, decay), (bb_o, kk * a), (k2_o, k2), (r_o, r), (v_o, v), (g_o, g),
                     (bon_o, bonus)):
        ref[...] = val.reshape(ref.shape)


def _rwkv_pre(z3, zfirst, p):
    b, t, _ = z3.shape
    if t >= 256:
        bt, tt = 1, 256
    else:
        bt, tt = min(b, 256 // t), t
    tpb = tt // 8
    row = lambda bi, j: (bi, j, 0)
    const2 = lambda bi, j: (0, 0)
    vec = pl.BlockSpec((1, RW_WIDTH), const2)
    low = pl.BlockSpec((LANES, RW_WIDTH), const2)
    out_spec = pl.BlockSpec((bt, tt, RW_WIDTH), row)
    out_shape = jax.ShapeDtypeStruct((b, t, RW_WIDTH), F32)
    return pl.pallas_call(
        _rwkv_pre_body,
        grid=(b // bt, t // tt),
        in_specs=[pl.BlockSpec((bt, tt, SEG), row),
                  pl.BlockSpec((bt, 8, SEG), lambda bi, j: (bi, jnp.maximum(j * tpb - 1, 0), 0)),
                  pl.BlockSpec((bt, 1, SEG), lambda bi, j: (bi, 0, 0)),
                  pl.BlockSpec((1, 1, SEG), lambda bi, j: (0, 0, 0)),
                  vec, low, vec, low, low, vec, vec, vec],
        out_specs=[out_spec] * 8,
        out_shape=[out_shape] * 8,
        compiler_params=_cparams(("arbitrary", "arbitrary")),
        name="rwkv_pre",
    )(z3, z3, zfirst, p["mu"], p["w0"], p["w2"], p["a0"], p["a2"], p["g2"],
      p["k_k"], p["k_a"], p["r_k"])


def _rwkv_scan_body(nkk_ref, w_ref, bb_ref, k2_ref, r_ref, vc_ref, s0_ref, o_ref, sf_ref, s_scr, *, tc):
    j = pl.program_id(1)

    @pl.when(j == 0)
    def _():
        s_scr[...] = s0_ref[0]

    def halfsum(parts):
        while len(parts) > 1:
            parts = [parts[i] + parts[i + 1] for i in range(0, len(parts), 2)]
        return parts[0] + pltpu.roll(parts[0], LANES // 2, axis=1)

    def step(t, carry):
        nkk = nkk_ref[0, t]
        wt = w_ref[0, t]
        bbt = bb_ref[0, t]
        k2t = k2_ref[0, t]
        rt = r_ref[0, t]
        vcol = vc_ref[0, t]
        accs = [None] * SCAN_ACCS
        for kh in range(SCAN_KH):
            p = s_scr[kh] * nkk[kh:kh + 1]
            accs[kh % SCAN_ACCS] = p if kh < SCAN_ACCS else accs[kh % SCAN_ACCS] + p
        sa = halfsum(accs)
        accs = [None] * SCAN_ACCS
        for kh in range(SCAN_KH):
            s = s_scr[kh] * wt[kh:kh + 1] + sa * bbt[kh:kh + 1] + vcol * k2t[kh:kh + 1]
            s_scr[kh] = s
            q = s * rt[kh:kh + 1]
            accs[kh % SCAN_ACCS] = q if kh < SCAN_ACCS else accs[kh % SCAN_ACCS] + q
        o_ref[0, t] = halfsum(accs)
        return carry

    lax.fori_loop(0, tc, step, 0, unroll=SCAN_UNROLL if tc % SCAN_UNROLL == 0 else 1)

    @pl.when(j == pl.num_programs(1) - 1)
    def _():
        sf_ref[0] = s_scr[...]


def _rwkv_scan(rows, vcols, s0):
    g, t = vcols.shape[:2]
    tc = min(64, t)
    row = pl.BlockSpec((1, tc, SCAN_KH, LANES), lambda gi, j: (gi, j, 0, 0))
    col = pl.BlockSpec((1, tc, RW_HEAD_DIM, LANES), lambda gi, j: (gi, j, 0, 0))
    st = pl.BlockSpec((1, SCAN_KH, RW_HEAD_DIM, LANES), lambda gi, j: (gi, 0, 0, 0))
    return pl.pallas_call(
        functools.partial(_rwkv_scan_body, tc=tc),
        grid=(g, t // tc),
        in_specs=[row] * 5 + [col, st],
        out_specs=[col, st],
        out_shape=[jax.ShapeDtypeStruct(vcols.shape, F32), jax.ShapeDtypeStruct(s0.shape, F32)],
        scratch_shapes=[pltpu.VMEM((SCAN_KH, RW_HEAD_DIM, LANES), F32)],
        compiler_params=_cparams(("arbitrary", "arbitrary")),
        name="rwkv_scan",
    )(*rows, vcols, s0)


PACK_STEPS = LANES // SCAN_NB


def _slot_rows(q, ts):
    return pl.ds((q // SCAN_NB) * ts * SCAN_NB + q % SCAN_NB, ts, stride=SCAN_NB)


def _rows_to_step_lanes(x_ref, m_s, mt_s):
    nq, ts, _ = x_ref.shape
    for j in range(RW_WIDTH // LANES):
        for q in range(nq):
            m_s[j, _slot_rows(q, ts), :] = x_ref[q, :, j * LANES:(j + 1) * LANES]
        mt_s[j * LANES:(j + 1) * LANES, :] = m_s[j].T


def _step_lanes_to_rows(mt_s, m_s, o_ref):
    nq, ts, _ = o_ref.shape
    for j in range(RW_WIDTH // LANES):
        m_s[j] = mt_s[j * LANES:(j + 1) * LANES, :].T
        for q in range(nq):
            o_ref[q, :, j * LANES:(j + 1) * LANES] = m_s[j, _slot_rows(q, ts), :]


def _rot_groups(x, groups):
    groups %= PACK_STEPS
    return pltpu.roll(x, groups * SCAN_NB, axis=1) if groups else x


def _gather_rotated(tiles):
    n = PACK_STEPS
    group = lax.broadcasted_iota(jnp.int32, (1, LANES), 1) // SCAN_NB
    lst = [tiles[(-m) % n] for m in range(n)]
    for k in range(n.bit_length() - 1):
        bit = ((group >> k) & 1) == 1
        lst = [jnp.where(bit, lst[(s - (1 << k)) % n], lst[s]) for s in range(n)]
    return lst


def _pack_one(x_ref, y_ref, m_s, mt_s, nblk, dup):
    nrow = RW_WIDTH // nblk
    ts = y_ref.shape[1]
    _rows_to_step_lanes(x_ref, m_s, mt_s)
    zero = jnp.zeros((nrow, LANES), F32)
    tiles = [_rot_groups(mt_s[pl.ds(blk, nrow, stride=nblk), :], blk) if blk < nblk else zero
             for blk in range(PACK_STEPS)]
    for s, acc in enumerate(_gather_rotated(tiles)):
        y = _rot_groups(acc, -s)
        y_ref[s // ts, s % ts] = y + pltpu.roll(y, LANES // 2, axis=1) if dup else y


def _scan_pack_body(nkk_ref, w_ref, bb_ref, k2_ref, r_ref, v_ref,
                    nkk_o, w_o, bb_o, k2_o, r_o, vc_o, m_s, mt_s):
    pairs = ((nkk_ref, nkk_o), (w_ref, w_o), (bb_ref, bb_o), (k2_ref, k2_o), (r_ref, r_o))
    for i, (x_ref, y_ref) in enumerate(pairs):
        _pack_one(x_ref, y_ref, m_s.at[i], mt_s.at[i], 2 * RW_HEADS, False)
    _pack_one(v_ref, vc_o, m_s.at[5], mt_s.at[5], RW_HEADS, True)


def _scan_unpack_body(o_ref, out_ref, m_s, mt_s):
    m_s, mt_s = m_s.at[0], mt_s.at[0]
    ts = o_ref.shape[1]
    lane = lax.broadcasted_iota(jnp.int32, (1, LANES), 1)
    tiles = [_rot_groups(jnp.where(lane < LANES // 2, o_ref[s // ts, s % ts], 0.0), s) for s in range(PACK_STEPS)]
    for h, acc in enumerate(_gather_rotated(tiles)[:RW_HEADS]):
        mt_s[pl.ds(h, RW_HEAD_DIM, stride=RW_HEADS), :] = _rot_groups(acc, -h)
    _step_lanes_to_rows(mt_s, m_s, out_ref)


def _pack_scratch(n):
    return [pltpu.VMEM((n, RW_WIDTH // LANES, LANES, LANES), F32), pltpu.VMEM((n, RW_WIDTH, LANES), F32)]


def _pack_blocking(g, t):
    ts = min(t, PACK_STEPS)
    gp = PACK_STEPS // ts
    return (ts, gp) if t % ts == 0 and PACK_STEPS % ts == 0 and g % gp == 0 else None


def _scan_pack(arrs, ts, gp):
    b, t, _ = arrs[0].shape
    g = b // SCAN_NB
    rows = pl.BlockSpec((gp * SCAN_NB, ts, RW_WIDTH), lambda gi, j: (gi, j, 0))
    kspec = pl.BlockSpec((gp, ts, SCAN_KH, LANES), lambda gi, j: (gi, j, 0, 0))
    vspec = pl.BlockSpec((gp, ts, RW_HEAD_DIM, LANES), lambda gi, j: (gi, j, 0, 0))
    return pl.pallas_call(
        _scan_pack_body,
        grid=(g // gp, t // ts),
        in_specs=[rows] * 6,
        out_specs=[kspec] * 5 + [vspec],
        out_shape=[jax.ShapeDtypeStruct((g, t, SCAN_KH, LANES), F32)] * 5
        + [jax.ShapeDtypeStruct((g, t, RW_HEAD_DIM, LANES), F32)],
        scratch_shapes=_pack_scratch(6),
        compiler_params=_cparams(("arbitrary", "arbitrary")),
        name="scan_pack",
    )(*arrs)


def _scan_unpack(o, ts, gp):
    g, t = o.shape[:2]
    return pl.pallas_call(
        _scan_unpack_body,
        grid=(g // gp, t // ts),
        in_specs=[pl.BlockSpec((gp, ts, RW_HEAD_DIM, LANES), lambda gi, j: (gi, j, 0, 0))],
        out_specs=pl.BlockSpec((gp * SCAN_NB, ts, RW_WIDTH), lambda gi, j: (gi, j, 0)),
        out_shape=jax.ShapeDtypeStruct((g * SCAN_NB, t, RW_WIDTH), F32),
        scratch_shapes=_pack_scratch(1),
        compiler_params=_cparams(("arbitrary", "arbitrary")),
        name="scan_unpack",
    )(o)


def _seq_to_lanes(x):
    b, t, c = x.shape
    return x.reshape(b // SCAN_NB, SCAN_NB, t, c).transpose(0, 2, 3, 1)


def _scan_layouts(nkk, w, bb, k2, r, v, wkv0):
    b, t, _ = v.shape
    g = b // SCAN_NB
    blocking = _pack_blocking(g, t)
    if blocking:
        *rows, vc = _scan_pack([nkk, w, bb, k2, r, v], *blocking)
    else:
        rows = [_seq_to_lanes(a).reshape(g, t, SCAN_KH, LANES) for a in (nkk, w, bb, k2, r)]
        vc = _seq_to_lanes(v).reshape(g, t, RW_HEAD_DIM, LANES // 2)
        vc = jnp.concatenate([vc, vc], axis=-1)
    if wkv0 is None:
        s0 = jnp.zeros((g, SCAN_KH, RW_HEAD_DIM, LANES), F32)
    else:
        s0 = wkv0.reshape(g, SCAN_NB, RW_HEADS, RW_HEAD_DIM, SCAN_KH, 2)
        s0 = s0.transpose(0, 4, 3, 5, 2, 1).reshape(g, SCAN_KH, RW_HEAD_DIM, LANES)
    return rows, vc, s0


def _scan_unlayout(o, sf):
    g, t = o.shape[:2]
    blocking = _pack_blocking(g, t)
    if blocking:
        o = _scan_unpack(o, *blocking)
    else:
        o = o[..., :LANES // 2].reshape(g, t, RW_WIDTH, SCAN_NB)
        o = o.transpose(0, 3, 1, 2).reshape(g * SCAN_NB, t, RW_WIDTH)
    sf = sf.reshape(g, SCAN_KH, RW_HEAD_DIM, 2, RW_HEADS, SCAN_NB)
    sf = sf.transpose(0, 5, 4, 2, 1, 3).reshape(g * SCAN_NB, RW_HEADS, RW_HEAD_DIM, RW_HEAD_DIM)
    return o, sf


def _hgrn_body(q_ref, f_ref, i_ref, g_ref, lbl_ref, ng_ref, st_ref, o_ref, so_ref,
               qin_s, kin_s, kout_s, d_s, *, chunk, per_seq, layer):
    rows = q_ref.shape[0]
    lg = lbl_ref[...]
    e = jnp.exp(lg - jnp.max(lg, axis=0, keepdims=True))
    sm = e / jnp.sum(e, axis=0, keepdims=True)
    lb = jnp.sum(sm[0:layer + 1], axis=0, keepdims=True)
    f = lb + (1.0 - lb) * _sigmoid(f_ref[...])
    logf = jnp.log(f)
    kf = 1.0 - f
    pos = lax.broadcasted_iota(jnp.int32, (rows, LANES), 0) % chunk
    cum = logf
    suf = logf
    s = 1
    while s < chunk:
        cum = cum + jnp.where(pos >= s, pltpu.roll(cum, s, axis=0), 0.0)
        suf = suf + jnp.where(pos < chunk - s, pltpu.roll(suf, rows - s, axis=0), 0.0)
        s *= 2
    qin_s[...] = q_ref[...] * jnp.exp(cum)
    kin_s[...] = kf * jnp.exp(-cum)
    kout_s[...] = kf * jnp.exp(suf - logf)
    d_s[...] = jnp.exp(cum + suf - logf)
    so_ref[...] = st_ref[...]

    nch = LANES // chunk
    ri = lax.broadcasted_iota(jnp.int32, (LANES, LANES), 0)
    ci = lax.broadcasted_iota(jnp.int32, (LANES, LANES), 1)
    causal = (ri // chunk == ci // chunk) & (ci <= ri)
    ng = ng_ref[...]

    def tile(ti, carry):
        r0 = pl.multiple_of(ti * LANES, LANES)
        qi = qin_s[pl.ds(r0, LANES), :]
        ki = kin_s[pl.ds(r0, LANES), :]
        ko = kout_s[pl.ds(r0, LANES), :]
        dd = d_s[pl.ds(r0, LANES), :]
        vv = i_ref[pl.ds(r0, LANES), :]
        qb = qi.astype(BF16)
        vb = vv.astype(BF16)
        att = jnp.where(causal, _dot_nt(qb, ki.astype(BF16)), 0.0)
        o = _dot(att.astype(BF16), vb)
        kot = ko.T
        if per_seq:
            sall = jnp.concatenate([so_ref[c] for c in range(nch)], axis=1).astype(BF16)
            big = _dot(qb, sall)
            inter = jnp.zeros((LANES, LANES), F32)
            for c in range(nch):
                inter = inter + jnp.where(ri // chunk == c, big[:, c * LANES:(c + 1) * LANES], 0.0)
            o = o + inter
        dcols = [jnp.broadcast_to(dd[c * chunk:c * chunk + 1, :], (LANES, LANES)).T for c in range(nch)]
        kvs = [_dot(jnp.where(ci // chunk == c, kot, 0.0).astype(BF16), vb) for c in range(nch)]
        if per_seq:
            for c in range(nch):
                so_ref[c] = so_ref[c] * dcols[c] + kvs[c]
        else:
            st = so_ref[0]
            parts = []
            for c in range(nch):
                parts.append(_dot(qb[c * chunk:(c + 1) * chunk], st.astype(BF16)))
                st = st * dcols[c] + kvs[c]
            so_ref[0] = st
            o = o + jnp.concatenate(parts, axis=0)
        o = o * lax.rsqrt(jnp.mean(o * o, axis=-1, keepdims=True) + NORM_EPS) * ng
        gate = g_ref[pl.ds(r0, LANES), :]
        o_ref[pl.ds(r0, LANES), :] = o * (gate * _sigmoid(gate))
        return carry

    ntile = rows // LANES
    lax.fori_loop(0, ntile, tile, 0, unroll=HG_TILE_UNROLL if ntile % HG_TILE_UNROLL == 0 else 1)


def _hgrn(z2, lb_logits, norm_g, st0, *, b, t, layer):
    chunk = min(HG_CHUNK, t)
    per_seq = t < LANES
    if per_seq:
        rows = LANES
        ns = LANES // t
    else:
        rows = t
        ns = 1
    nblk = (b * t) // rows
    c0 = SEG // LANES

    def col(off):
        return pl.BlockSpec((rows, LANES), lambda g, h: (g, c0 + off * HG_HEADS + h))

    st_spec = pl.BlockSpec((ns, None, HG_DK, HG_DV), lambda g, h: (g, h, 0, 0))
    nd = lb_logits.shape[0]
    scr = pltpu.VMEM((rows, LANES), F32)
    return pl.pallas_call(
        functools.partial(_hgrn_body, chunk=chunk, per_seq=per_seq, layer=layer),
        grid=(nblk, HG_HEADS),
        in_specs=[col(0), col(1), col(2), col(3),
                  pl.BlockSpec((nd, LANES), lambda g, h: (0, h)),
                  pl.BlockSpec((1, LANES), lambda g, h: (0, 0)),
                  st_spec],
        out_specs=[pl.BlockSpec((rows, LANES), lambda g, h: (g, h)), st_spec],
        out_shape=[jax.ShapeDtypeStruct((b * t, HG_HEADS * HG_DV), F32),
                   jax.ShapeDtypeStruct(st0.shape, F32)],
        scratch_shapes=[scr, scr, scr, scr],
        compiler_params=_cparams(("arbitrary", "arbitrary")),
        name="hgrn",
    )(z2, z2, z2, z2, lb_logits, norm_g, st0)


def _merge_body(x_ref, o_ref, bon_ref, g_ref, ob_ref, gate_ref, lnw_ref, lnb_ref,
                wa_ref, wb_ref, wo_ref, nf_ref, wq_ref, keys_ref,
                x1_o, xn_o, st_o):
    o = o_ref[...]
    d = o - _headsum(o) * (1.0 / RW_HEAD_DIM)
    var = _headsum(d * d) * (1.0 / RW_HEAD_DIM)
    oa = d * lax.rsqrt(var + RW_GN_EPS) * lnw_ref[...] + lnb_ref[...]
    oa = (oa + bon_ref[...]) * g_ref[...]
    ya = _dot(oa.astype(BF16), wa_ref[...])
    yb = _dot(ob_ref[...].astype(BF16), wb_ref[...])
    gate = gate_ref[...]
    merged = _sigmoid(gate[:, 0:D_MODEL]) * ya + _sigmoid(gate[:, D_MODEL:]) * yb
    x1 = x_ref[...] + _dot(merged.astype(BF16), wo_ref[...])
    x1_o[...] = x1
    xn = _rms(x1, nf_ref[...]).astype(BF16)
    xn_o[...] = xn
    for c in range(0, PEER_HEADS * PEER_QDIM, 512):
        q = _dot(xn, wq_ref[:, c:c + 512]).astype(BF16)
        for s in range(4):
            hc = c // LANES + s
            st_o[hc] = _dot_nt(keys_ref[hc], q[:, s * LANES:(s + 1) * LANES])


def _merge(x, o_raw, bonus, g, o_b, z2, p):
    n = x.shape[0]
    tm = min(MERGE_TOKENS, n)
    nhc = 2 * PEER_HEADS
    r512 = pl.BlockSpec((tm, RW_WIDTH), lambda i: (i, 0))
    r1024 = pl.BlockSpec((tm, D_MODEL), lambda i: (i, 0))

    def full(a):
        nd = a.ndim
        return pl.BlockSpec(a.shape, lambda i: (0,) * nd, pipeline_mode=pl.Buffered(1))

    ws = [p["ln_w"], p["ln_b"], p["w_up_a"], p["w_up_b"], p["w_out"], p["norm_ffn_g"], p["w_q"], p["keys"]]
    return pl.pallas_call(
        _merge_body,
        grid=(n // tm,),
        in_specs=[r1024, r512, r512, r512, r512,
                  pl.BlockSpec((tm, SEG), lambda i: (i, 2))] + [full(a) for a in ws],
        out_specs=[r1024, r1024, pl.BlockSpec((nhc, PEER_KEYS, tm), lambda i: (0, 0, i))],
        out_shape=[jax.ShapeDtypeStruct((n, D_MODEL), F32),
                   jax.ShapeDtypeStruct((n, D_MODEL), BF16),
                   jax.ShapeDtypeStruct((nhc, PEER_KEYS, n), F32)],
        compiler_params=_cparams(("arbitrary",)),
        name="merge",
    )(x, o_raw, bonus, g, o_b, z2, *ws)


def _top16(xs, payloads):
    nrow = xs[0].shape[0]
    rowi = lax.broadcasted_iota(jnp.int32, xs[0].shape, 0).astype(F32)
    slot = lax.broadcasted_iota(jnp.int32, (PEER_TOPK, LANES), 0)
    zero = jnp.zeros((PEER_TOPK, LANES), F32)

    sub = 8
    rowc = [rowi[i:i + sub] for i in range(0, nrow, sub)]

    def argmax_rows(x):
        pairs = [(x[i:i + sub], rc) for i, rc in zip(range(0, nrow, sub), rowc)]
        while len(pairs) > 1:
            nxt = []
            for j in range(0, len(pairs) - 1, 2):
                (av, ai), (bv, bi) = pairs[j], pairs[j + 1]
                keep = av >= bv
                nxt.append((jnp.where(keep, av, bv), jnp.where(keep, ai, bi)))
            if len(pairs) % 2:
                nxt.append(pairs[-1])
            pairs = nxt
        v, i = pairs[0]
        m = jnp.max(v, axis=0, keepdims=True)
        return m, jnp.min(jnp.where(v == m, i, float(nrow)), axis=0, keepdims=True)

    def it(r, carry):
        here = slot == r
        new = []
        for (x, vals, outs), pay in zip(carry, payloads):
            m, idx = argmax_rows(x)
            vals = jnp.where(here, m, vals)
            picked = (idx,) if pay is None else pay(idx)
            outs = tuple(jnp.where(here, p, o) for p, o in zip(picked, outs))
            new.append((jnp.where(rowi == idx, -jnp.inf, x), vals, outs))
        return tuple(new)

    init = tuple((x, zero, (zero,) if pay is None else (zero, zero)) for x, pay in zip(xs, payloads))
    return [(vals, outs) for _, vals, outs in lax.fori_loop(0, PEER_TOPK, it, init)]


def _pair_candidates(v1, i1, v2, i2):
    sub = 8
    row8 = lax.broadcasted_iota(jnp.int32, (sub, LANES), 0)
    cand = [v1[0:1] + v2]
    for k1 in range(1, sub):
        cand.append(jnp.where(row8 < PEER_TOPK // (k1 + 1), v1[k1:k1 + 1] + v2[0:sub], -jnp.inf))
    cand.append(v1[sub:] + v2[0:1])
    tail = PEER_TOPK + sub * (sub - 1)
    slot = lax.broadcasted_iota(jnp.int32, (PEER_TOPK, LANES), 0)

    def keys_of(idx):
        r = idx.astype(jnp.int32)
        mid = r - PEER_TOPK
        k1 = jnp.where(r < PEER_TOPK, 0, jnp.where(r < tail, 1 + (mid >> 3), r - tail + sub))
        k2 = jnp.where(r < PEER_TOPK, r, jnp.where(r < tail, mid & (sub - 1), 0))
        return (jnp.sum(jnp.where(slot == k1, i1, 0.0), axis=0, keepdims=True),
                jnp.sum(jnp.where(slot == k2, i2, 0.0), axis=0, keepdims=True))

    return jnp.concatenate(cand, axis=0), keys_of


TOPK_HEADS_PER_ROUND = 2


def _topk_body(st_ref, a_o, b_o, g_o, a_s, b_s, g_s):
    nh = TOPK_HEADS_PER_ROUND

    def group(gi, carry):
        halves = _top16([st_ref[2 * nh * gi + i] for i in range(2 * nh)], [None] * (2 * nh))
        cands = [_pair_candidates(halves[2 * i][0], halves[2 * i][1][0], halves[2 * i + 1][0], halves[2 * i + 1][1][0])
                 for i in range(nh)]
        picks = _top16([c[0] for c in cands], [c[1] for c in cands])
        for i, (cv, (ea, eb)) in enumerate(picks):
            ex = jnp.exp(cv - cv[0:1])
            r0 = pl.multiple_of((gi * nh + i) * PEER_TOPK, PEER_TOPK)
            a_s[pl.ds(r0, PEER_TOPK), :] = ea
            b_s[pl.ds(r0, PEER_TOPK), :] = eb
            g_s[pl.ds(r0, PEER_TOPK), :] = ex / jnp.sum(ex, axis=0, keepdims=True)
        return carry

    lax.fori_loop(0, PEER_HEADS // nh, group, 0)
    a_o[...] = a_s[...].T
    b_o[...] = b_s[...].T
    g_o[...] = g_s[...].T


def _topk(st):
    nhc, nk, n = st.shape
    out = jax.ShapeDtypeStruct((n, LANES), F32)
    ospec = pl.BlockSpec((LANES, LANES), lambda i: (i, 0))
    scr = pltpu.VMEM((LANES, LANES), F32)
    return pl.pallas_call(
        _topk_body,
        grid=(n // LANES,),
        in_specs=[pl.BlockSpec((nhc, nk, LANES), lambda i: (0, 0, i))],
        out_specs=[ospec] * 3,
        out_shape=[out] * 3,
        scratch_shapes=[scr] * 3,
        compiler_params=_cparams(("arbitrary",)),
        name="topk",
    )(st)


def _peer_body(x1_ref, xn_ref, a_ref, b_ref, gt_ref, u_ref, v_ref, nf_ref, y_ref, gate_s, acc_s, *, tb):
    c = pl.program_id(1)

    @pl.when(c == 0)
    def _():
        acc_s[...] = jnp.zeros_like(acc_s)
        sub = lax.broadcasted_iota(jnp.int32, (PEER_KEYS, LANES), 0).astype(F32)

        def tok(t, carry):
            ar = a_ref[pl.ds(t, 1), :]
            br = b_ref[pl.ds(t, 1), :]
            gr = gt_ref[pl.ds(t, 1), :]
            oa = jnp.where(sub == ar, 1.0, 0.0).astype(BF16)
            ob = jnp.where(sub == br, gr, 0.0).astype(BF16)
            gate_s[pl.ds(pl.multiple_of(t * PEER_KEYS, PEER_KEYS), PEER_KEYS), :] = _dot_nt(oa, ob)
            return carry

        lax.fori_loop(0, tb, tok, 0, unroll=64)

    xn = xn_ref[...]
    parts = []
    for k in range(PEER_STEP // PEER_CHUNK):
        u = _dot_nt(xn, u_ref[k * PEER_CHUNK:(k + 1) * PEER_CHUNK, :])
        act = 0.5 * u * (1.0 + lax.erf(u * (2.0 ** -0.5)))
        row = c * (PEER_STEP // PEER_KEYS) + 2 * k
        g0 = gate_s[pl.ds(row, tb, stride=PEER_KEYS), :]
        g1 = gate_s[pl.ds(row + 1, tb, stride=PEER_KEYS), :]
        parts.append((jnp.concatenate([g0, g1], axis=1) * act).astype(BF16))
    acc_s[...] += _dot(jnp.concatenate(parts, axis=1), v_ref[...])

    @pl.when(c == pl.num_programs(1) - 1)
    def _():
        y_ref[...] = _rms(x1_ref[...] + acc_s[...], nf_ref[...])


def _peer(x1, xn, a_idx, b_idx, gates, u_tab, v_tab, norm_g):
    n = x1.shape[0]
    tb = min(PEER_TOKENS, n)
    ne = u_tab.shape[0]
    row = lambda i, c: (i, 0)
    return pl.pallas_call(
        functools.partial(_peer_body, tb=tb),
        grid=(n // tb, ne // PEER_STEP),
        in_specs=[pl.BlockSpec((tb, D_MODEL), row),
                  pl.BlockSpec((tb, D_MODEL), row),
                  pl.BlockSpec((tb, LANES), row),
                  pl.BlockSpec((tb, LANES), row),
                  pl.BlockSpec((tb, LANES), row),
                  pl.BlockSpec((PEER_STEP, D_MODEL), lambda i, c: (c, 0)),
                  pl.BlockSpec((PEER_STEP, D_MODEL), lambda i, c: (c, 0)),
                  pl.BlockSpec((1, D_MODEL), lambda i, c: (0, 0))],
        out_specs=pl.BlockSpec((tb, D_MODEL), row),
        out_shape=jax.ShapeDtypeStruct((n, D_MODEL), F32),
        scratch_shapes=[pltpu.VMEM((tb * PEER_KEYS, LANES), F32),
                        pltpu.VMEM((tb, D_MODEL), F32)],
        compiler_params=_cparams(("arbitrary", "arbitrary")),
        name="peer",
    )(x1, xn, a_idx, b_idx, gates, u_tab, v_tab, norm_g)


def _pad_rows(a, n):
    return jnp.concatenate([a, jnp.zeros((n - a.shape[0],) + a.shape[1:], a.dtype)], axis=0)


def _layer_params(l, w_in, rw_mu, rw_w0, rw_w2, rw_a0, rw_a2, rw_g2, rw_k_k, rw_k_a, rw_r_k, rw_ln_w,
                  rw_ln_b, w_up_a, w_up_b, w_out, norm_ffn_g, peer_w_q, peer_keys):
    hm = jnp.asarray(_HM)
    r0, k0, v0 = 0, RW_WIDTH, 2 * RW_WIDTH
    d0 = 3 * RW_WIDTH
    a0 = d0 + RW_DECAY_RANK
    g0 = a0 + RW_AAA_RANK

    def rw_layout(m):
        z64 = jnp.zeros(m.shape[:-1] + (64,), m.dtype)
        zend = jnp.zeros(m.shape[:-1] + (SEG - RW_PAD,), m.dtype)
        return jnp.concatenate([m[..., r0:k0][..., hm], m[..., k0:v0][..., hm], m[..., v0:d0][..., hm],
                                m[..., d0:a0], z64, m[..., a0:g0], z64, m[..., g0:RW_COLS], zend], axis=-1)

    w = w_in[l]
    w_all = jnp.concatenate([rw_layout(w[:, :RW_COLS]), w[:, RW_COLS:]], axis=1).astype(BF16)
    row = lambda a: a.reshape(1, -1)
    return {
        "w_all": w_all,
        "mu": rw_layout(rw_mu[l]).reshape(1, 1, SEG),
        "w0": row(rw_w0[l][hm]),
        "w2": _pad_rows(rw_w2[l][:, hm], LANES).astype(BF16),
        "a0": row(rw_a0[l][hm]),
        "a2": _pad_rows(rw_a2[l][:, hm], LANES).astype(BF16),
        "g2": rw_g2[l][:, hm].astype(BF16),
        "k_k": row(rw_k_k[l][hm]),
        "k_a": row(rw_k_a[l][hm]),
        "r_k": row(rw_r_k[l].reshape(-1)[hm]),
        "ln_w": row(rw_ln_w[l][hm]),
        "ln_b": row(rw_ln_b[l][hm]),
        "w_up_a": w_up_a[l][hm, :].astype(BF16),
        "w_up_b": w_up_b[l].astype(BF16),
        "w_out": w_out[l].astype(BF16),
        "norm_ffn_g": row(norm_ffn_g[l]),
        "w_q": peer_w_q[l].astype(BF16),
        "keys": peer_keys[l].reshape(2 * PEER_HEADS, PEER_KEYS, PEER_QDIM // 2).astype(BF16),
    }


def _layer(l, x, shift0, wkv0, hg0, p, norm_mix_g, hg_lb_logits, hg_norm_g, u_tab, v_tab, out_norm_g):
    b, t, _ = x.shape
    n = b * t
    x2 = x.reshape(n, D_MODEL)
    gmix = norm_mix_g.reshape(1, D_MODEL)
    z2 = _proj(x2, gmix, p["w_all"], do_norm=True, nseg=3)
    new_shift = _rms_rows(x[:, -1], gmix)
    if shift0 is None:
        zfirst = jnp.zeros((b, 1, SEG), F32)
        h0 = jnp.zeros((b, HG_HEADS, HG_DK, HG_DV), F32)
    else:
        zfirst = _proj(shift0, gmix, p["w_all"], do_norm=False, nseg=1).reshape(b, 1, SEG)
        h0 = hg0
    nkk, w, bb, k2, r, v, g, bonus = _rwkv_pre(z2.reshape(b, t, 3 * SEG), zfirst, p)
    o_raw, new_wkv = _scan_unlayout(*_rwkv_scan(*_scan_layouts(nkk, w, bb, k2, r, v, wkv0)))
    o_b, new_hg = _hgrn(z2, hg_lb_logits, hg_norm_g.reshape(1, HG_DV), h0, b=b, t=t, layer=l)
    flat = lambda a: a.reshape(n, RW_WIDTH)
    x1, xn, st = _merge(x2, flat(o_raw), flat(bonus), flat(g), o_b, z2, p)
    a_idx, b_idx, gates = _topk(st)
    y = _peer(x1, xn, a_idx, b_idx, gates, u_tab, v_tab, out_norm_g.reshape(1, D_MODEL))
    return y.reshape(b, t, D_MODEL), new_shift, new_wkv, new_hg


def kernel(x_prompt, x_sample, state_rwkv_shift, state_rwkv_wkv, state_hgrn, norm_mix_g, w_in, rw_mu, rw_w0,
           rw_w2, rw_a0, rw_a2, rw_g2, rw_k_k, rw_k_a, rw_r_k, rw_ln_w, rw_ln_b, hg_lb_logits, hg_norm_g,
           w_up_a, w_up_b, w_out, norm_ffn_g, peer_w_q, peer_keys, peer_u, peer_v, norm_final_g):
    depth = w_in.shape[0]
    assert depth == 1, "the final rmsnorm is fused into the last layer's expert kernel"
    outs = []
    for l in range(depth):
        p = _layer_params(l, w_in, rw_mu, rw_w0, rw_w2, rw_a0, rw_a2, rw_g2, rw_k_k, rw_k_a, rw_r_k,
                          rw_ln_w, rw_ln_b, w_up_a, w_up_b, w_out, norm_ffn_g, peer_w_q, peer_keys)
        u_tab = peer_u[l].astype(BF16)
        v_tab = peer_v[l].astype(BF16)
        common = (p, norm_mix_g[l], hg_lb_logits, hg_norm_g[l], u_tab, v_tab, norm_final_g)
        yp, p_shift, p_wkv, p_hg = _layer(l, x_prompt, None, None, None, *common)
        ys, s_shift, s_wkv, s_hg = _layer(l, x_sample, state_rwkv_shift[l], state_rwkv_wkv[l],
                                          state_hgrn[l], *common)
        outs = (yp, ys, p_shift[None], p_wkv[None], p_hg[None], s_shift[None], s_wkv[None], s_hg[None])
    return outs
```
